```python
import math
import jax
import jax.numpy as jnp
from jax import lax
import numpy as np

D_MODEL = 2048
BATCH = 2
SEQ = 4096
DEPTH = 4

GRID_W = 64
CTX_LEN = 256
N_BRANCH = 4
BRANCH_W = D_MODEL // 2
CONV_K = 31
RWKV_HEAD = 64
RWKV_HEADS = BRANCH_W // RWKV_HEAD
DECAY_LORA = 96
AAA_LORA = 96
RWKV_GN_EPS = 64e-5
DIFF_HALF = 64
DIFF_VDIM = 2 * DIFF_HALF
DIFF_HEADS = BRANCH_W // DIFF_VDIM
GQA_HEAD = 128
GQA_HEADS = BRANCH_W // GQA_HEAD
GQA_KV_HEADS = GQA_HEADS // 4
BLOCK_Q = 128
ROPE_THETA = 10000.0
NORM_EPS = 1e-6

IN_SIZES = (
    BRANCH_W, BRANCH_W, BRANCH_W,
    BRANCH_W, BRANCH_W, BRANCH_W, DECAY_LORA, AAA_LORA, BRANCH_W,
    BRANCH_W, BRANCH_W, BRANCH_W, BRANCH_W,
    GQA_HEADS * GQA_HEAD, GQA_KV_HEADS * GQA_HEAD, GQA_KV_HEADS * GQA_HEAD, BRANCH_W,
    N_BRANCH * D_MODEL,
)
D_IN = sum(IN_SIZES)

kernel_name = "hybrid_conv_rwkv7_diffattn_gqa_flow_block"


def _rms_norm(t, g, eps=NORM_EPS):
    tf = t.astype(jnp.float32)
    y = tf * lax.rsqrt(jnp.mean(tf * tf, axis=-1, keepdims=True) + eps)
    return (y * g.astype(jnp.float32)).astype(t.dtype)


def _layer_norm(t, g, b, eps=1e-5):
    tf = t.astype(jnp.float32)
    mu = jnp.mean(tf, axis=-1, keepdims=True)
    var = jnp.mean(jnp.square(tf - mu), axis=-1, keepdims=True)
    return ((tf - mu) * lax.rsqrt(var + eps) * g.astype(jnp.float32) + b.astype(jnp.float32)).astype(t.dtype)


def _axial_rope(pos_row, pos_col, dim):
    half = dim // 2
    inv = ROPE_THETA ** (-jnp.arange(0, half, 2, dtype=jnp.float32) / half)
    ang = jnp.concatenate([pos_row[:, None].astype(jnp.float32) * inv,
                           pos_col[:, None].astype(jnp.float32) * inv], axis=-1)
    return jnp.cos(ang), jnp.sin(ang)


def _rope(t, cos, sin):
    shp = (1, t.shape[1]) + (1,) * (t.ndim - 3) + (cos.shape[-1],)
    cs, sn = cos.reshape(shp), sin.reshape(shp)
    tp = t.astype(jnp.float32).reshape(t.shape[:-1] + (-1, 2))
    t1, t2 = tp[..., 0], tp[..., 1]
    return jnp.stack([t1 * cs - t2 * sn, t1 * sn + t2 * cs], axis=-1).reshape(t.shape).astype(t.dtype)


def _dwconv(u, w, b):
    out = lax.conv_general_dilated(
        u, w[:, None, :].astype(u.dtype), window_strides=(1,),
        padding=[(CONV_K // 2, CONV_K // 2)], dimension_numbers=('NWC', 'WIO', 'NWC'),
        feature_group_count=u.shape[-1])
    return out + b.astype(u.dtype)


def _rev_segments(t, n_ctx):
    return jnp.concatenate([jnp.flip(t[:, :n_ctx], axis=1), jnp.flip(t[:, n_ctx:], axis=1)], axis=1)


def _rwkv7_bidirectional(r, k, v, wl, al, n_ctx, w0, w_up, a0, a_up, k_k, k_a, r_k, gn_g, gn_b):
    f32 = jnp.float32
    B, N = r.shape[:2]
    H, E = RWKV_HEADS, RWKV_HEAD
    out_dtype = r.dtype
    r, k, v = (t.astype(f32).reshape(B, N, H, E) for t in (r, k, v))
    wl, al = wl.astype(f32), al.astype(f32)
    w_pre = w0.astype(f32)[:, None, None, :] + jnp.einsum('bnr,erc->ebnc', wl, w_up.astype(f32))
    decay = jnp.exp(-jnp.exp(-jax.nn.softplus(-w_pre) - 0.5)).reshape(2, B, N, H, E)
    a = jax.nn.sigmoid(a0.astype(f32)[:, None, None, :]
                       + jnp.einsum('bnr,erc->ebnc', al, a_up.astype(f32))).reshape(2, B, N, H, E)
    kk = k * k_k.astype(f32).reshape(H, E)
    kk = kk * lax.rsqrt(jnp.maximum(jnp.sum(kk * kk, axis=-1, keepdims=True), 1e-24))
    k_dir = k[None] * (1.0 + (a - 1.0) * k_a.astype(f32).reshape(H, E))

    def order(t0, t1):
        return jnp.stack([t0, _rev_segments(t1, n_ctx)])

    xs = (order(r, r), order(decay[0], decay[1]), order(k_dir[0], k_dir[1]),
          order(v, v), order(kk, kk), order(a[0], a[1]))
    xs = tuple(jnp.moveaxis(t, 2, 0) for t in xs)

    def step(S, inp):
        r_t, w_t, k_t, v_t, kk_t, a_t = inp
        sa = jnp.einsum('ebhvk,ebhk->ebhv', S, -kk_t)
        S = (S * w_t[..., None, :] + sa[..., :, None] * (kk_t * a_t)[..., None, :]
             + v_t[..., :, None] * k_t[..., None, :])
        return S, jnp.einsum('ebhvk,ebhk->ebhv', S, r_t)

    S0 = jnp.zeros((2, B, H, E, E), f32)
    _, ys = lax.scan(step, S0, xs)
    ys = jnp.moveaxis(ys, 0, 2)
    y = ys[0] + _rev_segments(ys[1], n_ctx)
    mu = jnp.mean(y, axis=-1, keepdims=True)
    var = jnp.mean(jnp.square(y - mu), axis=-1, keepdims=True)
    y = ((y - mu) * lax.rsqrt(var + RWKV_GN_EPS)).reshape(B, N, H * E)
    y = y * gn_g.astype(f32) + gn_b.astype(f32)
    bonus = jnp.sum(jnp.sum(r[None] * k_dir * r_k.astype(f32), axis=-1, keepdims=True), axis=0) * v
    return (y + bonus.reshape(B, N, H * E)).astype(out_dtype)


def _latent_blocks(fn, q_lat):
    B, L = q_lat.shape[:2]
    nb = L // BLOCK_Q
    qb = jnp.swapaxes(q_lat.reshape((B, nb, BLOCK_Q) + q_lat.shape[2:]), 0, 1)
    ob = lax.map(fn, qb)
    return jnp.swapaxes(ob, 0, 1).reshape((B, L) + ob.shape[3:])


def _diff_core(q, k, v, lam):
    s = jnp.einsum('bqhjd,bkhjd->bhjqk', q.astype(jnp.float32), k.astype(jnp.float32)) * DIFF_HALF ** -0.5
    p = jax.nn.softmax(s, axis=-1)
    p = p[:, :, 0] - lam * p[:, :, 1]
    return jnp.einsum('bhqk,bkhd->bqhd', p.astype(v.dtype), v)


def _gqa_core(q, k, v):
    s = jnp.einsum('bqgrd,bkgd->bgrqk', q.astype(jnp.float32), k.astype(jnp.float32)) * GQA_HEAD ** -0.5
    p = jax.nn.softmax(s, axis=-1)
    return jnp.einsum('bgrqk,bkgd->bqgrd', p.astype(v.dtype), v)


def setup_inputs(seed: int = 0) -> dict:
    key = jax.random.key(seed)
    ks = list(jax.random.split(key, 32))
    f32 = jnp.float32
    L, D, W = DEPTH, D_MODEL, BRANCH_W

    def nrm(i, shape, scale):
        return jax.random.normal(ks[i], shape, f32) * scale

    return {
        'x': nrm(0, (BATCH, SEQ, D), 1.0),
        'c': nrm(1, (BATCH, D), 1.0),
        'ctx': nrm(2, (BATCH, CTX_LEN, D), 1.0),
        'c_ctx': nrm(3, (D,), 1.0),
        'norm_pre_g': 1.0 + nrm(4, (L, D), 0.02),
        'norm_post_g': 1.0 + nrm(5, (L, D), 0.02),
        'w_mod': nrm(6, (L, D, 3 * D), 0.5 * D ** -0.5),
        'b_mod': nrm(7, (L, 3 * D), 0.01),
        'w_in': nrm(8, (L, D, D_IN), D ** -0.5),
        'conv_w': nrm(9, (L, CONV_K, W), CONV_K ** -0.5),
        'conv_b': nrm(10, (L, W), 0.01),
        'conv_ln_g': 1.0 + nrm(11, (L, W), 0.02),
        'conv_ln_b': nrm(12, (L, W), 0.01),
        'rwkv_w0': jax.random.uniform(ks[13], (L, 2, W), f32, -6.0, 1.0),
        'rwkv_w_up': nrm(14, (L, 2, DECAY_LORA, W), 0.1 * DECAY_LORA ** -0.5),
        'rwkv_a0': nrm(15, (L, 2, W), 0.1),
        'rwkv_a_up': nrm(16, (L, 2, AAA_LORA, W), 0.1 * AAA_LORA ** -0.5),
        'rwkv_k_k': 0.85 + nrm(17, (L, W), 0.05),
        'rwkv_k_a': 1.0 + nrm(18, (L, W), 0.05),
        'rwkv_r_k': nrm(19, (L, RWKV_HEADS, RWKV_HEAD), 0.1),
        'rwkv_gn_g': 1.0 + nrm(20, (L, W), 0.02),
        'rwkv_gn_b': nrm(21, (L, W), 0.01),
        'diff_lam': nrm(22, (L, 4, DIFF_HALF), 0.1),
        'diff_subln_g': 1.0 + nrm(23, (L, DIFF_VDIM), 0.02),
        'gqa_qk_norm_g': 1.0 + nrm(24, (L, 2, GQA_HEAD), 0.02),
        'w_branch': nrm(25, (L, N_BRANCH, W, D), W ** -0.5),
        'b_gate': nrm(26, (L, N_BRANCH, D), 0.01),
        'w_out': nrm(27, (L, D, D), D ** -0.5),
    }


def reference(x, c, ctx, c_ctx, norm_pre_g, norm_post_g, w_mod, b_mod, w_in, conv_w, conv_b,
              conv_ln_g, conv_ln_b, rwkv_w0, rwkv_w_up, rwkv_a0, rwkv_a_up, rwkv_k_k, rwkv_k_a,
              rwkv_r_k, rwkv_gn_g, rwkv_gn_b, diff_lam, diff_subln_g, gqa_qk_norm_g, w_branch,
              b_gate, w_out):
    B, n_lat = x.shape[0], x.shape[1]
    n_ctx = ctx.shape[1]
    rows = n_lat // GRID_W
    pos_row = jnp.repeat(jnp.arange(rows), GRID_W)
    pos_col = jnp.broadcast_to(jnp.arange(GRID_W)[None, :], (rows, GRID_W)).reshape(-1)
    cos_d, sin_d = _axial_rope(pos_row, pos_col, DIFF_HALF)
    cos_g, sin_g = _axial_rope(pos_row, pos_col, GQA_HEAD)
    split_at = np.cumsum(IN_SIZES)[:-1].tolist()
    n_rep = GQA_HEADS // GQA_KV_HEADS

    x_ctx, x_lat = ctx, x
    for li in range(DEPTH):
        mod_lat = jax.nn.silu(c) @ w_mod[li] + b_mod[li]
        mod_ctx = jax.nn.silu(c_ctx) @ w_mod[li] + b_mod[li]
        sh_l, sc_l, gt_l = jnp.split(mod_lat[:, None, :], 3, axis=-1)
        sh_c, sc_c, gt_c = jnp.split(mod_ctx, 3, axis=-1)
        h = jnp.concatenate([_rms_norm(x_ctx, norm_pre_g[li]) * (1.0 + sc_c) + sh_c,
                             _rms_norm(x_lat, norm_pre_g[li]) * (1.0 + sc_l) + sh_l], axis=1)
        N = h.shape[1]
        (cv_val, cv_glu, cv_gate, rk_r, rk_k, rk_v, rk_wl, rk_al, rk_gate,
         df_q, df_k, df_v, df_gate, gq_q, gq_k, gq_v, gq_gate, merge_logits) = jnp.split(
            h @ w_in[li], split_at, axis=-1)

        u = cv_val * jax.nn.sigmoid(cv_glu)
        u = jnp.concatenate([_dwconv(u[:, :n_ctx], conv_w[li], conv_b[li]),
                             _dwconv(u[:, n_ctx:], conv_w[li], conv_b[li])], axis=1)
        br_conv = jax.nn.silu(_layer_norm(u, conv_ln_g[li], conv_ln_b[li])) * jax.nn.silu(cv_gate)

        br_rwkv = _rwkv7_bidirectional(
            rk_r, rk_k, rk_v, jnp.tanh(rk_wl), rk_al, n_ctx, rwkv_w0[li], rwkv_w_up[li], rwkv_a0[li],
            rwkv_a_up[li], rwkv_k_k[li], rwkv_k_a[li], rwkv_r_k[li], rwkv_gn_g[li], rwkv_gn_b[li]
        ) * jax.nn.silu(rk_gate)

        lam_p = diff_lam[li].astype(jnp.float32)
        lam_init = 0.8 - 0.6 * math.exp(-0.3 * li)
        lam = jnp.exp(jnp.sum(lam_p[0] * lam_p[1])) - jnp.exp(jnp.sum(lam_p[2] * lam_p[3])) + lam_init
        dq = df_q.reshape(B, N, DIFF_HEADS, 2, DIFF_HALF)
        dk = df_k.reshape(B, N, DIFF_HEADS, 2, DIFF_HALF)
        dv = df_v.reshape(B, N, DIFF_HEADS, DIFF_VDIM)
        dk_all = jnp.concatenate([dk[:, :n_ctx], _rope(dk[:, n_ctx:], cos_d, sin_d)], axis=1)
        do_ctx = _diff_core(dq[:, :n_ctx], dk[:, :n_ctx], dv[:, :n_ctx], lam)
        do_lat = _latent_blocks(lambda qb: _diff_core(qb, dk_all, dv, lam), _rope(dq[:, n_ctx:], cos_d, sin_d))
        do = jnp.concatenate([do_ctx, do_lat], axis=1)
        br_diff = (_rms_norm(do, diff_subln_g[li]) * (1.0 - lam_init)).reshape(B, N, BRANCH_W) * jax.nn.silu(df_gate)

        gq = _rms_norm(gq_q.reshape(B, N, GQA_HEADS, GQA_HEAD), gqa_qk_norm_g[li, 0])
        gk = _rms_norm(gq_k.reshape(B, N, GQA_KV_HEADS, GQA_HEAD), gqa_qk_norm_g[li, 1])
        gv = gq_v.reshape(B, N, GQA_KV_HEADS, GQA_HEAD)
        gk_all = jnp.concatenate([gk[:, :n_ctx], _rope(gk[:, n_ctx:], cos_g, sin_g)], axis=1)
        gq_grp = gq.reshape(B, N, GQA_KV_HEADS, n_rep, GQA_HEAD)
        go_ctx = _gqa_core(gq_grp[:, :n_ctx], gk[:, :n_ctx], gv[:, :n_ctx])
        go_lat = _latent_blocks(lambda qb: _gqa_core(qb, gk_all, gv), _rope(gq_grp[:, n_ctx:], cos_g, sin_g))
        br_gqa = jnp.concatenate([go_ctx, go_lat], axis=1).reshape(B, N, GQA_HEADS * GQA_HEAD) * jax.nn.silu(gq_gate)

        br = jnp.stack([br_conv, br_rwkv, br_diff, br_gqa], axis=2)
        gates = jax.nn.sigmoid(merge_logits.reshape(B, N, N_BRANCH, D_MODEL) + b_gate[li])
        merged = jnp.sum(jnp.einsum('bnjc,jcd->bnjd', br, w_branch[li]) * gates, axis=2)
        y = _rms_norm(merged @ w_out[li], norm_post_g[li])
        x_ctx = x_ctx + gt_c * y[:, :n_ctx]
        x_lat = x_lat + gt_l * y[:, n_ctx:]
    return x_lat
```

```python
import functools
import math

import jax
import jax.numpy as jnp
import numpy as np
from jax import lax
from jax.experimental import pallas as pl
from jax.experimental.pallas import tpu as pltpu

F32 = jnp.float32
BF16 = jnp.bfloat16

D_MODEL = 2048
BRANCH_W = D_MODEL // 2
N_BRANCH = 4
GRID_W = 64
CONV_K = 31
CONV_HALO = 16
RWKV_HEAD = 64
RWKV_HEADS = BRANCH_W // RWKV_HEAD
DECAY_LORA = 96
AAA_LORA = 96
LORA_PAD = 256
RWKV_GN_EPS = 64e-5
DIFF_HALF = 64
DIFF_VDIM = 2 * DIFF_HALF
DIFF_HEADS = BRANCH_W // DIFF_VDIM
GQA_HEAD = 128
GQA_HEADS = BRANCH_W // GQA_HEAD
GQA_KV_HEADS = GQA_HEADS // 4
GQA_REP = GQA_HEADS // GQA_KV_HEADS
ROPE_THETA = 10000.0
NORM_EPS = 1e-6

TOK = 256
LANES = 128
MOD_ROWS = 8
VMEM_LIMIT = 60 * 1024 * 1024

C_MERGE = 0
C_CONV = C_MERGE + N_BRANCH * D_MODEL
C_RK_R = C_CONV + 3 * BRANCH_W
C_RK_K = C_RK_R + BRANCH_W
C_RK_V = C_RK_K + BRANCH_W
C_RK_G = C_RK_V + BRANCH_W
C_DF_Q = C_RK_G + BRANCH_W
C_DF_K = C_DF_Q + BRANCH_W
C_DF_V = C_DF_K + BRANCH_W
C_DF_G = C_DF_V + BRANCH_W
C_GQ_Q = C_DF_G + BRANCH_W
C_GQ_G = C_GQ_Q + BRANCH_W
C_GQ_K = C_GQ_G + BRANCH_W
C_GQ_V = C_GQ_K + GQA_KV_HEADS * GQA_HEAD
C_LORA = C_GQ_V + GQA_KV_HEADS * GQA_HEAD
D_PROJ = C_LORA + LORA_PAD
O_LORA = 6 * BRANCH_W
O_RK_G = O_LORA + DECAY_LORA + AAA_LORA
O_GQ_K = O_RK_G + 6 * BRANCH_W
O_GQ_G = O_GQ_K + 2 * GQA_KV_HEADS * GQA_HEAD
O_MERGE = O_GQ_G + BRANCH_W


def _cparams(sem):
    return pltpu.CompilerParams(dimension_semantics=sem, vmem_limit_bytes=VMEM_LIMIT)


def _silu(t):
    return t * jax.nn.sigmoid(t)


def _split_dot(a, b_hi, b_lo=None):
    a_hi = a.astype(BF16)
    a_lo = (a - a_hi.astype(F32)).astype(BF16)
    out = jnp.dot(a_hi, b_hi, preferred_element_type=F32) + jnp.dot(a_lo, b_hi, preferred_element_type=F32)
    if b_lo is not None:
        out = out + jnp.dot(a_hi, b_lo, preferred_element_type=F32)
    return out


def _mod_kernel(c_ref, w_ref, b_ref, o_ref):
    cv = c_ref[...]
    s = _silu(cv).astype(BF16)
    o_ref[...] = jnp.dot(s, w_ref[...].astype(BF16), preferred_element_type=F32) + b_ref[...]


def _modulation(cvec, w_mod, b_mod):
    L = w_mod.shape[0]
    tn = 1536
    out = pl.pallas_call(
        _mod_kernel,
        grid=(L, 3 * D_MODEL // tn),
        in_specs=[
            pl.BlockSpec((MOD_ROWS, D_MODEL), lambda l, j: (0, 0)),
            pl.BlockSpec((None, D_MODEL, tn), lambda l, j: (l, 0, j)),
            pl.BlockSpec((None, 1, tn), lambda l, j: (l, 0, j)),
        ],
        out_specs=pl.BlockSpec((None, MOD_ROWS, tn), lambda l, j: (l, 0, j)),
        out_shape=jax.ShapeDtypeStruct((L, MOD_ROWS, 3 * D_MODEL), F32),
        compiler_params=_cparams(("arbitrary", "arbitrary")),
        name="modulation",
    )(cvec, w_mod, b_mod.reshape(L, 1, 3 * D_MODEL))
    return out.reshape(L, MOD_ROWS, 1, 3 * D_MODEL)


def _mod_row(batch):
    return lambda b, i: (jnp.where(i == 0, batch, b), 0, 0)


def _prenorm_kernel(x_ref, sh_ref, sc_ref, g_ref, h_ref):
    x = x_ref[...]
    y = x * lax.rsqrt(jnp.mean(x * x, axis=-1, keepdims=True) + NORM_EPS) * g_ref[...]
    h_ref[...] = (y * (1.0 + sc_ref[...]) + sh_ref[...]).astype(h_ref.dtype)


def _prenorm(xa, mod_l, g):
    B, N, _ = xa.shape
    row = _mod_row(B)
    return pl.pallas_call(
        _prenorm_kernel,
        grid=(B, N // TOK),
        in_specs=[
            pl.BlockSpec((None, TOK, D_MODEL), lambda b, i: (b, i, 0)),
            pl.BlockSpec((None, 1, D_MODEL), lambda b, i: row(b, i)[:2] + (0,)),
            pl.BlockSpec((None, 1, D_MODEL), lambda b, i: row(b, i)[:2] + (1,)),
            pl.BlockSpec((1, D_MODEL), lambda b, i: (0, 0)),
        ],
        out_specs=pl.BlockSpec((None, TOK, D_MODEL), lambda b, i: (b, i, 0)),
        out_shape=jax.ShapeDtypeStruct((B, N, D_MODEL), BF16),
        compiler_params=_cparams(("arbitrary", "arbitrary")),
        name="prenorm",
    )(xa, mod_l, mod_l, g.reshape(1, D_MODEL))


def _mm_kernel(a_ref, w_ref, o_ref):
    o_ref[...] = jnp.dot(a_ref[...], w_ref[...], preferred_element_type=F32).astype(o_ref.dtype)


def _project(h, w_all, li, col0, ncols, tn, out_dtype, tm):
    M, K = h.shape
    off = col0 // tn
    return pl.pallas_call(
        _mm_kernel,
        grid=(ncols // tn, M // tm),
        in_specs=[
            pl.BlockSpec((tm, K), lambda j, i: (i, 0)),
            pl.BlockSpec((None, K, tn), lambda j, i: (li, 0, j + off)),
        ],
        out_specs=pl.BlockSpec((tm, tn), lambda j, i: (i, j)),
        out_shape=jax.ShapeDtypeStruct((M, ncols), out_dtype),
        compiler_params=_cparams(("arbitrary", "arbitrary")),
        name="in_proj",
    )(h, w_all)


def _conv_kernel(cur_ref, gate_ref, prev_ref, next_ref, cw_ref, cb_ref, lg_ref, lb_ref, o_ref, buf_ref, *, n_tiles):
    i = pl.program_id(1)

    def glu(t):
        return t[:, :BRANCH_W].astype(F32) * jax.nn.sigmoid(t[:, BRANCH_W:2 * BRANCH_W].astype(F32))

    has_prev = jnp.logical_and(i != 0, i != 1)
    has_next = jnp.logical_and(i != 0, i != n_tiles - 1)
    cur = cur_ref[...]
    buf_ref[0:CONV_HALO, :] = jnp.where(has_prev, glu(prev_ref[...]), 0.0)
    buf_ref[CONV_HALO:CONV_HALO + TOK, :] = glu(cur)
    buf_ref[CONV_HALO + TOK:, :] = jnp.where(has_next, glu(next_ref[...]), 0.0)
    acc = jnp.broadcast_to(cb_ref[...], (TOK, BRANCH_W))
    base = CONV_HALO - CONV_K // 2
    for j in range(CONV_K):
        acc = acc + cw_ref[j:j + 1, :] * buf_ref[base + j:base + j + TOK, :]
    mu = jnp.mean(acc, axis=-1, keepdims=True)
    d = acc - mu
    var = jnp.mean(d * d, axis=-1, keepdims=True)
    y = d * lax.rsqrt(var + 1e-5) * lg_ref[...] + lb_ref[...]
    o_ref[...] = (_silu(y) * _silu(gate_ref[...].astype(F32))).astype(o_ref.dtype)


def _conv_branch(proj, conv_w, conv_b, ln_g, ln_b):
    B, N, _ = proj.shape
    nt = N // TOK
    hpt = TOK // CONV_HALO
    last = N // CONV_HALO - 1
    cblk = C_CONV // (2 * BRANCH_W)
    return pl.pallas_call(
        functools.partial(_conv_kernel, n_tiles=nt),
        grid=(B, nt),
        in_specs=[
            pl.BlockSpec((None, TOK, 2 * BRANCH_W), lambda b, i: (b, i, cblk)),
            pl.BlockSpec((None, TOK, BRANCH_W), lambda b, i: (b, i, 2 * cblk + 2)),
            pl.BlockSpec((None, CONV_HALO, 2 * BRANCH_W), lambda b, i: (b, jnp.maximum(i * hpt - 1, 0), cblk)),
            pl.BlockSpec((None, CONV_HALO, 2 * BRANCH_W), lambda b, i: (b, jnp.minimum((i + 1) * hpt, last), cblk)),
            pl.BlockSpec((CONV_K, BRANCH_W), lambda b, i: (0, 0)),
            pl.BlockSpec((1, BRANCH_W), lambda b, i: (0, 0)),
            pl.BlockSpec((1, BRANCH_W), lambda b, i: (0, 0)),
            pl.BlockSpec((1, BRANCH_W), lambda b, i: (0, 0)),
        ],
        out_specs=pl.BlockSpec((None, TOK, BRANCH_W), lambda b, i: (b, i, 0)),
        out_shape=jax.ShapeDtypeStruct((B, N, BRANCH_W), BF16),
        scratch_shapes=[pltpu.VMEM((TOK + 2 * CONV_HALO, BRANCH_W), F32)],
        compiler_params=_cparams(("arbitrary", "arbitrary")),
        name="conv_branch",
    )(proj, proj, proj, proj, conv_w, conv_b.reshape(1, -1), ln_g.reshape(1, -1), ln_b.reshape(1, -1))


def _rope_tables(n_ctx, n_lat, dim, reps):
    half = dim // 2
    t = np.arange(n_lat)
    inv = ROPE_THETA ** (-np.arange(0, half, 2, dtype=np.float64) / half)
    ang = np.concatenate([(t // GRID_W)[:, None] * inv, (t % GRID_W)[:, None] * inv], axis=-1)
    ang = np.concatenate([np.zeros((n_ctx, half)), ang], axis=0)
    cos = np.repeat(np.cos(ang), 2, axis=-1)
    sin = np.repeat(np.sin(ang), 2, axis=-1) * np.tile(np.array([-1.0, 1.0]), half)
    return (jnp.asarray(np.tile(cos, (1, reps)), F32), jnp.asarray(np.tile(sin, (1, reps)), F32))


def _rope(t, cos, sin_signed):
    lane = lax.broadcasted_iota(jnp.int32, t.shape, 1)
    partner = jnp.where(lane % 2 == 0, pltpu.roll(t, LANES - 1, 1), pltpu.roll(t, 1, 1))
    return t * cos + partner * sin_signed


def _softmax_t(s):
    e = jnp.exp(s - jnp.max(s, axis=0, keepdims=True))
    return e, jnp.sum(e, axis=0, keepdims=True)


def _scores_t(k, q):
    return lax.dot_general(k, q, (((1,), (1,)), ((), ())), preferred_element_type=F32)


def _gqa_kernel(q_ref, k_ref, v_ref, gate_ref, cos_ref, sin_ref, g_ref, o_ref, kp_ref, vt_ref, *, n_ctx, n_all):
    r = pl.program_id(2)
    qi = pl.program_id(3)

    @pl.when(jnp.logical_and(r == 0, qi == 0))
    def _():
        k = k_ref[...].astype(F32)
        k = k * lax.rsqrt(jnp.mean(k * k, axis=-1, keepdims=True) + NORM_EPS) * g_ref[1:2, :]
        kp_ref[...] = _rope(k, cos_ref[...], sin_ref[...]).astype(BF16)
        vt_ref[...] = v_ref[...].astype(F32).T.astype(BF16)

    q = q_ref[...].astype(F32)
    q = q * lax.rsqrt(jnp.mean(q * q, axis=-1, keepdims=True) + NORM_EPS) * g_ref[0:1, :]
    rows = pl.ds(pl.multiple_of(qi * TOK, TOK), TOK)
    q = _rope(q, cos_ref[rows, :], sin_ref[rows, :]) * GQA_HEAD ** -0.5
    qb = q.astype(BF16)
    gate = _silu(gate_ref[...].astype(F32))

    def attend(nk):
        e, l = _softmax_t(_scores_t(kp_ref[0:nk, :], qb))
        ot = jnp.dot(vt_ref[:, 0:nk], e.astype(BF16), preferred_element_type=F32) / l
        o_ref[...] = (ot.T * gate).astype(o_ref.dtype)

    pl.when(qi == 0)(lambda: attend(n_ctx))
    pl.when(qi != 0)(lambda: attend(n_all))


def _gqa_branch(proj, qk_g, cos, sin, n_ctx):
    B, N, _ = proj.shape
    blk = lambda c0: c0 // GQA_HEAD
    return pl.pallas_call(
        functools.partial(_gqa_kernel, n_ctx=n_ctx, n_all=N),
        grid=(B, GQA_KV_HEADS, GQA_REP, N // TOK),
        in_specs=[
            pl.BlockSpec((None, TOK, GQA_HEAD), lambda b, g, r, i: (b, i, blk(C_GQ_Q) + g * GQA_REP + r)),
            pl.BlockSpec((None, N, GQA_HEAD), lambda b, g, r, i: (b, 0, blk(C_GQ_K) + g)),
            pl.BlockSpec((None, N, GQA_HEAD), lambda b, g, r, i: (b, 0, blk(C_GQ_V) + g)),
            pl.BlockSpec((None, TOK, GQA_HEAD), lambda b, g, r, i: (b, i, blk(C_GQ_G) + g * GQA_REP + r)),
            pl.BlockSpec((N, GQA_HEAD), lambda b, g, r, i: (0, 0)),
            pl.BlockSpec((N, GQA_HEAD), lambda b, g, r, i: (0, 0)),
            pl.BlockSpec((2, GQA_HEAD), lambda b, g, r, i: (0, 0)),
        ],
        out_specs=pl.BlockSpec((None, TOK, GQA_HEAD), lambda b, g, r, i: (b, i, g * GQA_REP + r)),
        out_shape=jax.ShapeDtypeStruct((B, N, BRANCH_W), BF16),
        scratch_shapes=[pltpu.VMEM((N, GQA_HEAD), BF16), pltpu.VMEM((GQA_HEAD, N), BF16)],
        compiler_params=_cparams(("arbitrary",) * 4),
        name="gqa_branch",
    )(proj, proj, proj, proj, cos, sin, qk_g)


def _diff_kernel(q_ref, k_ref, v_ref, gate_ref, cos_ref, sin_ref, lam_ref, sg_ref, o_ref, kp_ref, vt_ref,
                 *, n_ctx, n_all, lam_init):
    qi = pl.program_id(2)

    @pl.when(qi == 0)
    def _():
        kp_ref[...] = _rope(k_ref[...].astype(F32), cos_ref[...], sin_ref[...]).astype(BF16)
        vt_ref[...] = v_ref[...].astype(F32).T.astype(BF16)

    lp = lam_ref[...]
    lam = (jnp.exp(jnp.sum(lp[0:1] * lp[1:2], axis=-1, keepdims=True))
           - jnp.exp(jnp.sum(lp[2:3] * lp[3:4], axis=-1, keepdims=True)) + lam_init)
    rows = pl.ds(pl.multiple_of(qi * TOK, TOK), TOK)
    q = _rope(q_ref[...].astype(F32), cos_ref[rows, :], sin_ref[rows, :]) * DIFF_HALF ** -0.5
    lane = lax.broadcasted_iota(jnp.int32, q.shape, 1)
    q1 = jnp.where(lane < DIFF_HALF, q, 0.0).astype(BF16)
    q2 = jnp.where(lane >= DIFF_HALF, q, 0.0).astype(BF16)
    gate = _silu(gate_ref[...].astype(F32))

    def attend(nk):
        k = kp_ref[0:nk, :]
        e1, l1 = _softmax_t(_scores_t(k, q1))
        e2, l2 = _softmax_t(_scores_t(k, q2))
        p = e1 * (1.0 / l1) - e2 * (lam / l2)
        o = jnp.dot(vt_ref[:, 0:nk], p.astype(BF16), preferred_element_type=F32).T
        o = o * lax.rsqrt(jnp.mean(o * o, axis=-1, keepdims=True) + NORM_EPS) * sg_ref[...]
        o_ref[...] = (o * (1.0 - lam_init) * gate).astype(o_ref.dtype)

    pl.when(qi == 0)(lambda: attend(n_ctx))
    pl.when(qi != 0)(lambda: attend(n_all))


def _diff_branch(proj, lam_p, subln_g, cos, sin, n_ctx, lam_init):
    B, N, _ = proj.shape
    blk = lambda c0: c0 // DIFF_VDIM
    return pl.pallas_call(
        functools.partial(_diff_kernel, n_ctx=n_ctx, n_all=N, lam_init=lam_init),
        grid=(B, DIFF_HEADS, N // TOK),
        in_specs=[
            pl.BlockSpec((None, TOK, DIFF_VDIM), lambda b, h, i: (b, i, blk(C_DF_Q) + h)),
            pl.BlockSpec((None, N, DIFF_VDIM), lambda b, h, i: (b, 0, blk(C_DF_K) + h)),
            pl.BlockSpec((None, N, DIFF_VDIM), lambda b, h, i: (b, 0, blk(C_DF_V) + h)),
            pl.BlockSpec((None, TOK, DIFF_VDIM), lambda b, h, i: (b, i, blk(C_DF_G) + h)),
            pl.BlockSpec((N, DIFF_VDIM), lambda b, h, i: (0, 0)),
            pl.BlockSpec((N, DIFF_VDIM), lambda b, h, i: (0, 0)),
            pl.BlockSpec((4, DIFF_HALF), lambda b, h, i: (0, 0)),
            pl.BlockSpec((1, DIFF_VDIM), lambda b, h, i: (0, 0)),
        ],
        out_specs=pl.BlockSpec((None, TOK, DIFF_VDIM), lambda b, h, i: (b, i, h)),
        out_shape=jax.ShapeDtypeStruct((B, N, BRANCH_W), BF16),
        scratch_shapes=[pltpu.VMEM((N, DIFF_VDIM), BF16), pltpu.VMEM((DIFF_VDIM, N), BF16)],
        compiler_params=_cparams(("arbitrary",) * 3),
        name="diff_branch",
    )(proj, proj, proj, proj, cos, sin, lam_p, subln_g.reshape(1, DIFF_VDIM))


def _rwkv_prep_kernel(r_ref, k_ref, v_ref, lora_ref, wup_hi_ref, wup_lo_ref, aup_hi_ref, aup_lo_ref, w0_ref, a0_ref,
                      kk_g_ref, ka_ref, rk_ref, ones_ref,
                      r_o, kk_o, v_o, w_o, kka_o, kd_o, bonus_o):
    r = r_ref[...].astype(F32)
    k = k_ref[...].astype(F32)
    v = v_ref[...].astype(F32)
    lora = lora_ref[...]
    wl = jnp.tanh(lora)
    ones = ones_ref[...]
    kk = k * kk_g_ref[...]
    kk = kk * lax.rsqrt(jnp.maximum(_split_dot(kk * kk, ones), 1e-24))
    r_o[...] = r
    kk_o[...] = kk
    v_o[...] = v
    bsum = jnp.zeros_like(r)
    for e in range(2):
        w_pre = w0_ref[e:e + 1, :] + _split_dot(wl, wup_hi_ref[e], wup_lo_ref[e])
        w_o[e] = jnp.exp(-jax.nn.sigmoid(w_pre) * math.exp(-0.5))
        a = jax.nn.sigmoid(a0_ref[e:e + 1, :] + _split_dot(lora, aup_hi_ref[e], aup_lo_ref[e]))
        kd = k * (1.0 + (a - 1.0) * ka_ref[...])
        kd_o[e] = kd
        kka_o[e] = kk * a
        bsum = bsum + kd
    bonus_o[...] = _split_dot(r * bsum * rk_ref[...], ones) * v


def _rwkv_prep(proj, lora, wup, aup, w0, a0, k_k, k_a, r_k, ones):
    B, N, _ = proj.shape
    blk = lambda c0: c0 // BRANCH_W
    tok_spec = lambda c0: pl.BlockSpec((None, TOK, BRANCH_W), lambda b, i: (b, i, blk(c0)))
    const2 = lambda shape: pl.BlockSpec(shape, lambda b, i: (0,) * len(shape))
    nat = jax.ShapeDtypeStruct((B, N, BRANCH_W), F32)
    nat2 = jax.ShapeDtypeStruct((2, B, N, BRANCH_W), F32)
    o1 = pl.BlockSpec((None, TOK, BRANCH_W), lambda b, i: (b, i, 0))
    o2 = pl.BlockSpec((2, None, TOK, BRANCH_W), lambda b, i: (0, b, i, 0))
    split = lambda t: (t.astype(BF16), (t - t.astype(BF16).astype(F32)).astype(BF16))
    wup_hi, wup_lo = split(wup)
    aup_hi, aup_lo = split(aup)
    return pl.pallas_call(
        _rwkv_prep_kernel,
        grid=(B, N // TOK),
        in_specs=[
            tok_spec(C_RK_R), tok_spec(C_RK_K), tok_spec(C_RK_V),
            pl.BlockSpec((None, TOK, LORA_PAD), lambda b, i: (b, i, 0)),
            const2((2, LORA_PAD, BRANCH_W)), const2((2, LORA_PAD, BRANCH_W)),
            const2((2, LORA_PAD, BRANCH_W)), const2((2, LORA_PAD, BRANCH_W)),
            const2((2, BRANCH_W)), const2((2, BRANCH_W)),
            const2((1, BRANCH_W)), const2((1, BRANCH_W)), const2((1, BRANCH_W)),
            const2((BRANCH_W, BRANCH_W)),
        ],
        out_specs=[o1, o1, o1, o2, o2, o2, o1],
        out_shape=[nat, nat, nat, nat2, nat2, nat2, nat],
        compiler_params=_cparams(("arbitrary", "arbitrary")),
        name="rwkv_prep",
    )(proj, proj, proj, lora, wup_hi, wup_lo, aup_hi, aup_lo, w0, a0,
      k_k.reshape(1, -1), k_a.reshape(1, -1), r_k.reshape(1, -1), ones)


SCAN_T = 64
SCAN_V = RWKV_HEAD // 2


def _rwkv_scan_kernel(w_ref, kka_ref, kd_ref, r_ref, kkn_ref, v_ref, o_ref, s_ref, sa_ref):
    @pl.when(pl.program_id(0) == 0)
    def _():
        s_ref[...] = jnp.zeros_like(s_ref)
        sa_ref[...] = jnp.zeros_like(sa_ref)

    def step(t, sa):
        v_t = v_ref[t]
        o = jnp.zeros((SCAN_V, LANES), F32)
        nsa = jnp.zeros((SCAN_V, LANES), F32)
        for k in range(RWKV_HEAD):
            row = pl.ds(k, 1)
            s = s_ref[k] * w_ref[t, row, :] + sa * kka_ref[t, row, :] + v_t * kd_ref[t, row, :]
            s_ref[k] = s
            o = o + s * r_ref[t, row, :]
            nsa = nsa + s * kkn_ref[t, row, :]
        o_ref[t] = o
        return -nsa

    sa_ref[...] = lax.fori_loop(0, SCAN_T, step, sa_ref[...])


def _rwkv_scan(w, kka, kd, r, kkn, v):
    N = w.shape[0]
    kspec = pl.BlockSpec((SCAN_T, RWKV_HEAD, LANES), lambda i: (i, 0, 0))
    vspec = pl.BlockSpec((SCAN_T, SCAN_V, LANES), lambda i: (i, 0, 0))
    return pl.pallas_call(
        _rwkv_scan_kernel,
        grid=(N // SCAN_T,),
        in_specs=[kspec] * 5 + [vspec],
        out_specs=vspec,
        out_shape=jax.ShapeDtypeStruct((N, SCAN_V, LANES), F32),
        scratch_shapes=[pltpu.VMEM((RWKV_HEAD, SCAN_V, LANES), F32), pltpu.VMEM((SCAN_V, LANES), F32)],
        compiler_params=_cparams(("arbitrary",)),
        name="rwkv_scan",
    )(w, kka, kd, r, kkn, v)


def _rwkv_post_kernel(y0_ref, y1_ref, bonus_ref, gate_ref, g_ref, b_ref, ones_ref, o_ref):
    y = y0_ref[...] + y1_ref[...]
    ones = ones_ref[...]
    mu = _split_dot(y, ones) * (1.0 / RWKV_HEAD)
    d = y - mu
    var = _split_dot(d * d, ones) * (1.0 / RWKV_HEAD)
    yn = d * lax.rsqrt(var + RWKV_GN_EPS) * g_ref[...] + b_ref[...]
    o_ref[...] = ((yn + bonus_ref[...]) * _silu(gate_ref[...].astype(F32))).astype(o_ref.dtype)


def _rwkv_post(y0, y1, bonus, proj, gn_g, gn_b, ones):
    B, N, _ = y0.shape
    nat = pl.BlockSpec((None, TOK, BRANCH_W), lambda b, i: (b, i, 0))
    const2 = lambda shape: pl.BlockSpec(shape, lambda b, i: (0, 0))
    return pl.pallas_call(
        _rwkv_post_kernel,
        grid=(B, N // TOK),
        in_specs=[nat, nat, nat,
                  pl.BlockSpec((None, TOK, BRANCH_W), lambda b, i: (b, i, C_RK_G // BRANCH_W)),
                  const2((1, BRANCH_W)), const2((1, BRANCH_W)), const2((BRANCH_W, BRANCH_W))],
        out_specs=nat,
        out_shape=jax.ShapeDtypeStruct((B, N, BRANCH_W), BF16),
        compiler_params=_cparams(("arbitrary", "arbitrary")),
        name="rwkv_post",
    )(y0, y1, bonus, proj, gn_g.reshape(1, -1), gn_b.reshape(1, -1), ones)


def _rev_segments(t, n_ctx, axis):
    ctx, lat = jnp.split(t, [n_ctx], axis=axis)
    return jnp.concatenate([jnp.flip(ctx, axis=axis), jnp.flip(lat, axis=axis)], axis=axis)


def _to_scan_keys(t0, t1, n_ctx):
    B, N, _ = t0.shape
    t = jnp.stack([t0, _rev_segments(t1, n_ctx, 1)])
    t = t.reshape(2, B, N, RWKV_HEADS, RWKV_HEAD).transpose(2, 4, 0, 1, 3).reshape(N, RWKV_HEAD, 2 * B * RWKV_HEADS)
    return jnp.concatenate([t, t], axis=-1)


def _to_scan_values(t, n_ctx):
    B, N, _ = t.shape
    t = jnp.stack([t, _rev_segments(t, n_ctx, 1)])
    t = t.reshape(2, B, N, RWKV_HEADS, 2, SCAN_V).transpose(2, 5, 4, 0, 1, 3)
    return t.reshape(N, SCAN_V, 2 * 2 * B * RWKV_HEADS)


def _from_scan_values(o, B, n_ctx):
    N = o.shape[0]
    o = o.reshape(N, SCAN_V, 2, 2, B, RWKV_HEADS).transpose(3, 4, 0, 5, 2, 1).reshape(2, B, N, BRANCH_W)
    return o[0], _rev_segments(o[1], n_ctx, 1)


def _rwkv_branch(proj, lora, li, p, ones, n_ctx):
    B, N, _ = proj.shape
    assert 2 * 2 * B * RWKV_HEADS == LANES, "scan layout packs (value half, direction, sample, head) on the lanes"
    r, kk, v, w, kka, kd, bonus = _rwkv_prep(
        proj, lora, p["wup"][li], p["aup"][li], p["rwkv_w0"][li], p["rwkv_a0"][li],
        p["rwkv_k_k"][li], p["rwkv_k_a"][li], p["rwkv_r_k"][li], ones)
    kk_s = _to_scan_keys(kk, kk, n_ctx)
    kkn_s = jnp.concatenate([kk_s[1:], jnp.zeros_like(kk_s[:1])], axis=0)
    o = _rwkv_scan(_to_scan_keys(w[0], w[1], n_ctx), _to_scan_keys(kka[0], kka[1], n_ctx),
                   _to_scan_keys(kd[0], kd[1], n_ctx), _to_scan_keys(r, r, n_ctx), kkn_s,
                   _to_scan_values(v, n_ctx))
    y0, y1 = _from_scan_values(o, B, n_ctx)
    return _rwkv_post(y0, y1, bonus, proj, p["rwkv_gn_g"][li], p["rwkv_gn_b"][li], ones)


def _merge_kernel(bc_ref, br_ref, bd_ref, bq_ref, lg_ref, wb_ref, bg_ref, wo_ref, x_ref, gp_ref, gt_ref, o_ref):
    merged = jnp.zeros((TOK, D_MODEL), F32)
    for j, b_ref in enumerate((bc_ref, br_ref, bd_ref, bq_ref)):
        pj = jnp.dot(b_ref[...], wb_ref[j], preferred_element_type=F32)
        gj = jax.nn.sigmoid(lg_ref[:, j * D_MODEL:(j + 1) * D_MODEL].astype(F32) + bg_ref[j:j + 1, :])
        merged = merged + pj * gj
    y = jnp.dot(merged.astype(BF16), wo_ref[...], preferred_element_type=F32)
    y = y * lax.rsqrt(jnp.mean(y * y, axis=-1, keepdims=True) + NORM_EPS) * gp_ref[...]
    o_ref[...] = x_ref[...] + gt_ref[...] * y


def _merge(branches, proj, w_branch, b_gate, w_out, li, xa, g_post, mod_l):
    B, N, _ = xa.shape
    row = _mod_row(B)
    bspec = pl.BlockSpec((None, TOK, BRANCH_W), lambda b, i: (b, i, 0))
    xspec = pl.BlockSpec((None, TOK, D_MODEL), lambda b, i: (b, i, 0))
    once = pl.Buffered(1)
    return pl.pallas_call(
        _merge_kernel,
        grid=(B, N // TOK),
        in_specs=[bspec] * 4 + [
            pl.BlockSpec((None, TOK, N_BRANCH * D_MODEL), lambda b, i: (b, i, C_MERGE // (N_BRANCH * D_MODEL))),
            pl.BlockSpec((None, N_BRANCH, BRANCH_W, D_MODEL), lambda b, i: (li, 0, 0, 0), pipeline_mode=once),
            pl.BlockSpec((None, N_BRANCH, D_MODEL), lambda b, i: (li, 0, 0)),
            pl.BlockSpec((None, D_MODEL, D_MODEL), lambda b, i: (li, 0, 0), pipeline_mode=once),
            xspec,
            pl.BlockSpec((1, D_MODEL), lambda b, i: (0, 0)),
            pl.BlockSpec((None, 1, D_MODEL), lambda b, i: row(b, i)[:2] + (2,)),
        ],
        out_specs=xspec,
        out_shape=jax.ShapeDtypeStruct((B, N, D_MODEL), F32),
        compiler_params=_cparams(("arbitrary", "arbitrary")),
        name="merge",
    )(*branches, proj, w_branch, b_gate, w_out, xa, g_post.reshape(1, -1), mod_l)


def kernel(x, c, ctx, c_ctx, norm_pre_g, norm_post_g, w_mod, b_mod, w_in, conv_w, conv_b, conv_ln_g, conv_ln_b,
           rwkv_w0, rwkv_w_up, rwkv_a0, rwkv_a_up, rwkv_k_k, rwkv_k_a, rwkv_r_k, rwkv_gn_g, rwkv_gn_b,
           diff_lam, diff_subln_g, gqa_qk_norm_g, w_branch, b_gate, w_out):
    B, n_lat, _ = x.shape
    n_ctx = ctx.shape[1]
    depth = w_in.shape[0]
    N = n_ctx + n_lat
    assert n_ctx == TOK and n_lat % TOK == 0 and B + 1 <= MOD_ROWS

    pad = jnp.zeros(w_in.shape[:2] + (LORA_PAD - DECAY_LORA - AAA_LORA,), w_in.dtype)
    w_in_r = jnp.concatenate(
        [w_in[..., O_MERGE:], w_in[..., :O_LORA], w_in[..., O_RK_G:O_GQ_K], w_in[..., O_GQ_G:O_MERGE],
         w_in[..., O_GQ_K:O_GQ_G], w_in[..., O_LORA:O_RK_G], pad], axis=-1).astype(BF16)
    assert w_in_r.shape[-1] == D_PROJ
    w_branch_b = w_branch.astype(BF16)
    w_out_b = w_out.astype(BF16)
    zrows = lambda t, lo: jnp.pad(t, ((0, 0), (0, 0), (lo, LORA_PAD - lo - t.shape[2]), (0, 0)))
    params = dict(wup=zrows(rwkv_w_up, 0), aup=zrows(rwkv_a_up, DECAY_LORA), rwkv_w0=rwkv_w0, rwkv_a0=rwkv_a0,
                  rwkv_k_k=rwkv_k_k, rwkv_k_a=rwkv_k_a, rwkv_r_k=rwkv_r_k.reshape(depth, BRANCH_W),
                  rwkv_gn_g=rwkv_gn_g, rwkv_gn_b=rwkv_gn_b)
    head = np.arange(BRANCH_W) // RWKV_HEAD
    ones = jnp.asarray(head[:, None] == head[None, :], BF16)
    cos_d, sin_d = _rope_tables(n_ctx, n_lat, DIFF_HALF, 2)
    cos_g, sin_g = _rope_tables(n_ctx, n_lat, GQA_HEAD, 1)

    cvec = jnp.concatenate([c, c_ctx[None, :], jnp.zeros((MOD_ROWS - B - 1, D_MODEL), c.dtype)], axis=0)
    mod = _modulation(cvec, w_mod, b_mod)
    xa = jnp.concatenate([ctx, x], axis=1)

    for li in range(depth):
        h = _prenorm(xa, mod[li], norm_pre_g[li]).reshape(B * N, D_MODEL)
        tm = 1088 if (B * N) % 1088 == 0 else TOK
        proj = _project(h, w_in_r, li, 0, C_LORA, 512, BF16, tm)
        lora = _project(h, w_in_r, li, C_LORA, LORA_PAD, LORA_PAD, F32, tm)
        proj = proj.reshape(B, N, C_LORA)
        lora = lora.reshape(B, N, LORA_PAD)
        lam_init = 0.8 - 0.6 * math.exp(-0.3 * li)
        br_conv = _conv_branch(proj, conv_w[li], conv_b[li], conv_ln_g[li], conv_ln_b[li])
        br_rwkv = _rwkv_branch(proj, lora, li, params, ones, n_ctx)
        br_diff = _diff_branch(proj, diff_lam[li], diff_subln_g[li], cos_d, sin_d, n_ctx, lam_init)
        br_gqa = _gqa_branch(proj, gqa_qk_norm_g[li], cos_g, sin_g, n_ctx)
        xa = _merge((br_conv, br_rwkv, br_diff, br_gqa), proj, w_branch_b, b_gate, w_out_b, li, xa,
                    norm_post_g[li], mod[li])
    return xa[:, n_ctx:]
```

```python
import functools
import math

import jax
import jax.numpy as jnp
import numpy as np
from jax import lax
from jax.experimental import pallas as pl
from jax.experimental.pallas import tpu as pltpu

F32 = jnp.float32
BF16 = jnp.bfloat16

D_MODEL = 2048
BRANCH_W = D_MODEL // 2
N_BRANCH = 4
GRID_W = 64
CONV_K = 31
CONV_HALO = 16
RWKV_HEAD = 64
RWKV_HEADS = BRANCH_W // RWKV_HEAD
DECAY_LORA = 96
AAA_LORA = 96
LORA_PAD = 256
RWKV_GN_EPS = 64e-5
DIFF_HALF = 64
DIFF_VDIM = 2 * DIFF_HALF
DIFF_HEADS = BRANCH_W // DIFF_VDIM
GQA_HEAD = 128
GQA_HEADS = BRANCH_W // GQA_HEAD
GQA_KV_HEADS = GQA_HEADS // 4
GQA_REP = GQA_HEADS // GQA_KV_HEADS
ROPE_THETA = 10000.0
NORM_EPS = 1e-6

TOK = 256
LANES = 128
MOD_ROWS = 8
VMEM_LIMIT = 60 * 1024 * 1024

C_MERGE = 0
C_CONV = C_MERGE + N_BRANCH * D_MODEL
C_RK_R = C_CONV + 3 * BRANCH_W
C_RK_K = C_RK_R + BRANCH_W
C_RK_V = C_RK_K + BRANCH_W
C_RK_G = C_RK_V + BRANCH_W
C_DF_Q = C_RK_G + BRANCH_W
C_DF_K = C_DF_Q + BRANCH_W
C_DF_V = C_DF_K + BRANCH_W
C_DF_G = C_DF_V + BRANCH_W
C_GQ_Q = C_DF_G + BRANCH_W
C_GQ_G = C_GQ_Q + BRANCH_W
C_GQ_K = C_GQ_G + BRANCH_W
C_GQ_V = C_GQ_K + GQA_KV_HEADS * GQA_HEAD
C_LORA = C_GQ_V + GQA_KV_HEADS * GQA_HEAD
D_PROJ = C_LORA + LORA_PAD
O_LORA = 6 * BRANCH_W
O_RK_G = O_LORA + DECAY_LORA + AAA_LORA
O_GQ_K = O_RK_G + 6 * BRANCH_W
O_GQ_G = O_GQ_K + 2 * GQA_KV_HEADS * GQA_HEAD
O_MERGE = O_GQ_G + BRANCH_W


def _cparams(sem):
    return pltpu.CompilerParams(dimension_semantics=sem, vmem_limit_bytes=VMEM_LIMIT)


def _silu(t):
    return t * jax.nn.sigmoid(t)


def _split_dot(a, b_hi, b_lo=None):
    a_hi = a.astype(BF16)
    a_lo = (a - a_hi.astype(F32)).astype(BF16)
    out = jnp.dot(a_hi, b_hi, preferred_element_type=F32) + jnp.dot(a_lo, b_hi, preferred_element_type=F32)
    if b_lo is not None:
        out = out + jnp.dot(a_hi, b_lo, preferred_element_type=F32)
    return out


def _mod_kernel(c_ref, w_ref, b_ref, o_ref):
    cv = c_ref[...]
    s = _silu(cv).astype(BF16)
    o_ref[...] = jnp.dot(s, w_ref[...].astype(BF16), preferred_element_type=F32) + b_ref[...]


def _modulation(cvec, w_mod, b_mod):
    L = w_mod.shape[0]
    tn = 1536
    out = pl.pallas_call(
        _mod_kernel,
        grid=(L, 3 * D_MODEL // tn),
        in_specs=[
            pl.BlockSpec((MOD_ROWS, D_MODEL), lambda l, j: (0, 0)),
            pl.BlockSpec((None, D_MODEL, tn), lambda l, j: (l, 0, j)),
            pl.BlockSpec((None, 1, tn), lambda l, j: (l, 0, j)),
        ],
        out_specs=pl.BlockSpec((None, MOD_ROWS, tn), lambda l, j: (l, 0, j)),
        out_shape=jax.ShapeDtypeStruct((L, MOD_ROWS, 3 * D_MODEL), F32),
        compiler_params=_cparams(("arbitrary", "arbitrary")),
        name="modulation",
    )(cvec, w_mod, b_mod.reshape(L, 1, 3 * D_MODEL))
    return out.reshape(L, MOD_ROWS, 1, 3 * D_MODEL)


def _mod_row(batch):
    return lambda b, i: (jnp.where(i == 0, batch, b), 0, 0)


def _prenorm_kernel(x_ref, sh_ref, sc_ref, g_ref, h_ref):
    x = x_ref[...]
    y = x * lax.rsqrt(jnp.mean(x * x, axis=-1, keepdims=True) + NORM_EPS) * g_ref[...]
    h_ref[...] = (y * (1.0 + sc_ref[...]) + sh_ref[...]).astype(h_ref.dtype)


def _prenorm(xa, mod_l, g):
    B, N, _ = xa.shape
    row = _mod_row(B)
    return pl.pallas_call(
        _prenorm_kernel,
        grid=(B, N // TOK),
        in_specs=[
            pl.BlockSpec((None, TOK, D_MODEL), lambda b, i: (b, i, 0)),
            pl.BlockSpec((None, 1, D_MODEL), lambda b, i: row(b, i)[:2] + (0,)),
            pl.BlockSpec((None, 1, D_MODEL), lambda b, i: row(b, i)[:2] + (1,)),
            pl.BlockSpec((1, D_MODEL), lambda b, i: (0, 0)),
        ],
        out_specs=pl.BlockSpec((None, TOK, D_MODEL), lambda b, i: (b, i, 0)),
        out_shape=jax.ShapeDtypeStruct((B, N, D_MODEL), BF16),
        compiler_params=_cparams(("arbitrary", "arbitrary")),
        name="prenorm",
    )(xa, mod_l, mod_l, g.reshape(1, D_MODEL))


def _mm_kernel(a_ref, w_ref, o_ref):
    o_ref[...] = jnp.dot(a_ref[...], w_ref[...], preferred_element_type=F32).astype(o_ref.dtype)


def _project(h, w_all, li, col0, ncols, tn, out_dtype, tm):
    M, K = h.shape
    off = col0 // tn
    return pl.pallas_call(
        _mm_kernel,
        grid=(ncols // tn, M // tm),
        in_specs=[
            pl.BlockSpec((tm, K), lambda j, i: (i, 0)),
            pl.BlockSpec((None, K, tn), lambda j, i: (li, 0, j + off)),
        ],
        out_specs=pl.BlockSpec((tm, tn), lambda j, i: (i, j)),
        out_shape=jax.ShapeDtypeStruct((M, ncols), out_dtype),
        compiler_params=_cparams(("arbitrary", "arbitrary")),
        name="in_proj",
    )(h, w_all)


def _conv_kernel(cur_ref, gate_ref, prev_ref, next_ref, cw_ref, cb_ref, lg_ref, lb_ref, o_ref, buf_ref, *, n_tiles):
    i = pl.program_id(1)

    def glu(t):
        return t[:, :BRANCH_W].astype(F32) * jax.nn.sigmoid(t[:, BRANCH_W:2 * BRANCH_W].astype(F32))

    has_prev = jnp.logical_and(i != 0, i != 1)
    has_next = jnp.logical_and(i != 0, i != n_tiles - 1)
    cur = cur_ref[...]
    buf_ref[0:CONV_HALO, :] = jnp.where(has_prev, glu(prev_ref[...]), 0.0)
    buf_ref[CONV_HALO:CONV_HALO + TOK, :] = glu(cur)
    buf_ref[CONV_HALO + TOK:, :] = jnp.where(has_next, glu(next_ref[...]), 0.0)
    acc = jnp.broadcast_to(cb_ref[...], (TOK, BRANCH_W))
    base = CONV_HALO - CONV_K // 2
    for j in range(CONV_K):
        acc = acc + cw_ref[j:j + 1, :] * buf_ref[base + j:base + j + TOK, :]
    mu = jnp.mean(acc, axis=-1, keepdims=True)
    d = acc - mu
    var = jnp.mean(d * d, axis=-1, keepdims=True)
    y = d * lax.rsqrt(var + 1e-5) * lg_ref[...] + lb_ref[...]
    o_ref[...] = (_silu(y) * _silu(gate_ref[...].astype(F32))).astype(o_ref.dtype)


def _conv_branch(proj, conv_w, conv_b, ln_g, ln_b):
    B, N, _ = proj.shape
    nt = N // TOK
    hpt = TOK // CONV_HALO
    last = N // CONV_HALO - 1
    cblk = C_CONV // (2 * BRANCH_W)
    return pl.pallas_call(
        functools.partial(_conv_kernel, n_tiles=nt),
        grid=(B, nt),
        in_specs=[
            pl.BlockSpec((None, TOK, 2 * BRANCH_W), lambda b, i: (b, i, cblk)),
            pl.BlockSpec((None, TOK, BRANCH_W), lambda b, i: (b, i, 2 * cblk + 2)),
            pl.BlockSpec((None, CONV_HALO, 2 * BRANCH_W), lambda b, i: (b, jnp.maximum(i * hpt - 1, 0), cblk)),
            pl.BlockSpec((None, CONV_HALO, 2 * BRANCH_W), lambda b, i: (b, jnp.minimum((i + 1) * hpt, last), cblk)),
            pl.BlockSpec((CONV_K, BRANCH_W), lambda b, i: (0, 0)),
            pl.BlockSpec((1, BRANCH_W), lambda b, i: (0, 0)),
            pl.BlockSpec((1, BRANCH_W), lambda b, i: (0, 0)),
            pl.BlockSpec((1, BRANCH_W), lambda b, i: (0, 0)),
        ],
        out_specs=pl.BlockSpec((None, TOK, BRANCH_W), lambda b, i: (b, i, 0)),
        out_shape=jax.ShapeDtypeStruct((B, N, BRANCH_W), BF16),
        scratch_shapes=[pltpu.VMEM((TOK + 2 * CONV_HALO, BRANCH_W), F32)],
        compiler_params=_cparams(("arbitrary", "arbitrary")),
        name="conv_branch",
    )(proj, proj, proj, proj, conv_w, conv_b.reshape(1, -1), ln_g.reshape(1, -1), ln_b.reshape(1, -1))


def _rope_tables(n_ctx, n_lat, dim, reps):
    half = dim // 2
    t = np.arange(n_lat)
    inv = ROPE_THETA ** (-np.arange(0, half, 2, dtype=np.float64) / half)
    ang = np.concatenate([(t // GRID_W)[:, None] * inv, (t % GRID_W)[:, None] * inv], axis=-1)
    ang = np.concatenate([np.zeros((n_ctx, half)), ang], axis=0)
    cos = np.repeat(np.cos(ang), 2, axis=-1)
    sin = np.repeat(np.sin(ang), 2, axis=-1) * np.tile(np.array([-1.0, 1.0]), half)
    return (jnp.asarray(np.tile(cos, (1, reps)), F32), jnp.asarray(np.tile(sin, (1, reps)), F32))


def _rope(t, cos, sin_signed):
    lane = lax.broadcasted_iota(jnp.int32, t.shape, 1)
    partner = jnp.where(lane % 2 == 0, pltpu.roll(t, LANES - 1, 1), pltpu.roll(t, 1, 1))
    return t * cos + partner * sin_signed


LOG2_E = 1.4426950408889634


def _softmax_t(s):
    e = jnp.exp2(s - jnp.max(s, axis=0, keepdims=True))
    return e, jnp.sum(e, axis=0, keepdims=True)


def _scores_t(k, q):
    return lax.dot_general(k, q, (((1,), (1,)), ((), ())), preferred_element_type=F32)


def _gqa_kernel(q_ref, k_ref, v_ref, gate_ref, cos_ref, sin_ref, g_ref, o_ref, kp_ref, vt_ref, *, n_ctx, n_all):
    r = pl.program_id(2)
    qi = pl.program_id(3)

    @pl.when(jnp.logical_and(r == 0, qi == 0))
    def _():
        k = k_ref[...].astype(F32)
        k = k * lax.rsqrt(jnp.mean(k * k, axis=-1, keepdims=True) + NORM_EPS) * g_ref[1:2, :]
        kp_ref[...] = _rope(k, cos_ref[...], sin_ref[...]).astype(BF16)
        vt_ref[...] = v_ref[...].astype(F32).T.astype(BF16)

    q = q_ref[...].astype(F32)
    q = q * lax.rsqrt(jnp.mean(q * q, axis=-1, keepdims=True) + NORM_EPS) * g_ref[0:1, :]
    rows = pl.ds(pl.multiple_of(qi * TOK, TOK), TOK)
    q = _rope(q, cos_ref[rows, :], sin_ref[rows, :]) * (GQA_HEAD ** -0.5 * LOG2_E)
    qb = q.astype(BF16)
    gate = _silu(gate_ref[...].astype(F32))

    def attend(nk):
        e, l = _softmax_t(_scores_t(kp_ref[0:nk, :], qb))
        ot = jnp.dot(vt_ref[:, 0:nk], e.astype(BF16), preferred_element_type=F32) / l
        o_ref[...] = (ot.T * gate).astype(o_ref.dtype)

    pl.when(qi == 0)(lambda: attend(n_ctx))
    pl.when(qi != 0)(lambda: attend(n_all))


def _gqa_branch(proj, qk_g, cos, sin, n_ctx):
    B, N, _ = proj.shape
    blk = lambda c0: c0 // GQA_HEAD
    return pl.pallas_call(
        functools.partial(_gqa_kernel, n_ctx=n_ctx, n_all=N),
        grid=(B, GQA_KV_HEADS, GQA_REP, N // TOK),
        in_specs=[
            pl.BlockSpec((None, TOK, GQA_HEAD), lambda b, g, r, i: (b, i, blk(C_GQ_Q) + g * GQA_REP + r)),
            pl.BlockSpec((None, N, GQA_HEAD), lambda b, g, r, i: (b, 0, blk(C_GQ_K) + g)),
            pl.BlockSpec((None, N, GQA_HEAD), lambda b, g, r, i: (b, 0, blk(C_GQ_V) + g)),
            pl.BlockSpec((None, TOK, GQA_HEAD), lambda b, g, r, i: (b, i, blk(C_GQ_G) + g * GQA_REP + r)),
            pl.BlockSpec((N, GQA_HEAD), lambda b, g, r, i: (0, 0)),
            pl.BlockSpec((N, GQA_HEAD), lambda b, g, r, i: (0, 0)),
            pl.BlockSpec((2, GQA_HEAD), lambda b, g, r, i: (0, 0)),
        ],
        out_specs=pl.BlockSpec((None, TOK, GQA_HEAD), lambda b, g, r, i: (b, i, g * GQA_REP + r)),
        out_shape=jax.ShapeDtypeStruct((B, N, BRANCH_W), BF16),
        scratch_shapes=[pltpu.VMEM((N, GQA_HEAD), BF16), pltpu.VMEM((GQA_HEAD, N), BF16)],
        compiler_params=_cparams(("arbitrary",) * 4),
        name="gqa_branch",
    )(proj, proj, proj, proj, cos, sin, qk_g)


def _diff_kernel(q_ref, k_ref, v_ref, gate_ref, cos_ref, sin_ref, lam_ref, sg_ref, o_ref, kp_ref, vt_ref,
                 *, n_ctx, n_all, lam_init):
    qi = pl.program_id(2)

    @pl.when(qi == 0)
    def _():
        kp_ref[...] = _rope(k_ref[...].astype(F32), cos_ref[...], sin_ref[...]).astype(BF16)
        vt_ref[...] = v_ref[...].astype(F32).T.astype(BF16)

    lp = lam_ref[...]
    lam = (jnp.exp(jnp.sum(lp[0:1] * lp[1:2], axis=-1, keepdims=True))
           - jnp.exp(jnp.sum(lp[2:3] * lp[3:4], axis=-1, keepdims=True)) + lam_init)
    rows = pl.ds(pl.multiple_of(qi * TOK, TOK), TOK)
    q = _rope(q_ref[...].astype(F32), cos_ref[rows, :], sin_ref[rows, :]) * (DIFF_HALF ** -0.5 * LOG2_E)
    lane = lax.broadcasted_iota(jnp.int32, q.shape, 1)
    q1 = jnp.where(lane < DIFF_HALF, q, 0.0).astype(BF16)
    q2 = jnp.where(lane >= DIFF_HALF, q, 0.0).astype(BF16)
    gate = _silu(gate_ref[...].astype(F32))

    def attend(nk):
        k = kp_ref[0:nk, :]
        e1, l1 = _softmax_t(_scores_t(k, q1))
        e2, l2 = _softmax_t(_scores_t(k, q2))
        p = e1 - e2 * (lam * l1 / l2)
        o = (jnp.dot(vt_ref[:, 0:nk], p.astype(BF16), preferred_element_type=F32) / l1).T
        o = o * lax.rsqrt(jnp.mean(o * o, axis=-1, keepdims=True) + NORM_EPS) * sg_ref[...]
        o_ref[...] = (o * (1.0 - lam_init) * gate).astype(o_ref.dtype)

    pl.when(qi == 0)(lambda: attend(n_ctx))
    pl.when(qi != 0)(lambda: attend(n_all))


def _diff_branch(proj, lam_p, subln_g, cos, sin, n_ctx, lam_init):
    B, N, _ = proj.shape
    blk = lambda c0: c0 // DIFF_VDIM
    return pl.pallas_call(
        functools.partial(_diff_kernel, n_ctx=n_ctx, n_all=N, lam_init=lam_init),
        grid=(B, DIFF_HEADS, N // TOK),
        in_specs=[
            pl.BlockSpec((None, TOK, DIFF_VDIM), lambda b, h, i: (b, i, blk(C_DF_Q) + h)),
            pl.BlockSpec((None, N, DIFF_VDIM), lambda b, h, i: (b, 0, blk(C_DF_K) + h)),
            pl.BlockSpec((None, N, DIFF_VDIM), lambda b, h, i: (b, 0, blk(C_DF_V) + h)),
            pl.BlockSpec((None, TOK, DIFF_VDIM), lambda b, h, i: (b, i, blk(C_DF_G) + h)),
            pl.BlockSpec((N, DIFF_VDIM), lambda b, h, i: (0, 0)),
            pl.BlockSpec((N, DIFF_VDIM), lambda b, h, i: (0, 0)),
            pl.BlockSpec((4, DIFF_HALF), lambda b, h, i: (0, 0)),
            pl.BlockSpec((1, DIFF_VDIM), lambda b, h, i: (0, 0)),
        ],
        out_specs=pl.BlockSpec((None, TOK, DIFF_VDIM), lambda b, h, i: (b, i, h)),
        out_shape=jax.ShapeDtypeStruct((B, N, BRANCH_W), BF16),
        scratch_shapes=[pltpu.VMEM((N, DIFF_VDIM), BF16), pltpu.VMEM((DIFF_VDIM, N), BF16)],
        compiler_params=_cparams(("arbitrary",) * 3),
        name="diff_branch",
    )(proj, proj, proj, proj, cos, sin, lam_p, subln_g.reshape(1, DIFF_VDIM))


def _rwkv_prep_kernel(r_ref, k_ref, v_ref, lora_ref, wup_hi_ref, wup_lo_ref, aup_hi_ref, aup_lo_ref, w0_ref, a0_ref,
                      kk_g_ref, ka_ref, rk_ref, ones_ref,
                      r_o, kk_o, v_o, w_o, kka_o, kd_o, bonus_o):
    r = r_ref[...].astype(F32)
    k = k_ref[...].astype(F32)
    v = v_ref[...].astype(F32)
    lora = lora_ref[...]
    wl = jnp.tanh(lora)
    ones = ones_ref[...]
    kk = k * kk_g_ref[...]
    kk = kk * lax.rsqrt(jnp.maximum(_split_dot(kk * kk, ones), 1e-24))
    r_o[...] = r
    kk_o[...] = kk
    v_o[...] = v
    bsum = jnp.zeros_like(r)
    for e in range(2):
        w_pre = w0_ref[e:e + 1, :] + _split_dot(wl, wup_hi_ref[e], wup_lo_ref[e])
        w_o[e] = jnp.exp(-jax.nn.sigmoid(w_pre) * math.exp(-0.5))
        a = jax.nn.sigmoid(a0_ref[e:e + 1, :] + _split_dot(lora, aup_hi_ref[e], aup_lo_ref[e]))
        kd = k * (1.0 + (a - 1.0) * ka_ref[...])
        kd_o[e] = kd
        kka_o[e] = kk * a
        bsum = bsum + kd
    bonus_o[...] = _split_dot(r * bsum * rk_ref[...], ones) * v


def _rwkv_prep(proj, lora, wup, aup, w0, a0, k_k, k_a, r_k, ones):
    B, N, _ = proj.shape
    blk = lambda c0: c0 // BRANCH_W
    tok_spec = lambda c0: pl.BlockSpec((None, TOK, BRANCH_W), lambda b, i: (b, i, blk(c0)))
    const2 = lambda shape: pl.BlockSpec(shape, lambda b, i: (0,) * len(shape))
    nat = jax.ShapeDtypeStruct((B, N, BRANCH_W), F32)
    nat2 = jax.ShapeDtypeStruct((2, B, N, BRANCH_W), F32)
    o1 = pl.BlockSpec((None, TOK, BRANCH_W), lambda b, i: (b, i, 0))
    o2 = pl.BlockSpec((2, None, TOK, BRANCH_W), lambda b, i: (0, b, i, 0))
    split = lambda t: (t.astype(BF16), (t - t.astype(BF16).astype(F32)).astype(BF16))
    wup_hi, wup_lo = split(wup)
    aup_hi, aup_lo = split(aup)
    return pl.pallas_call(
        _rwkv_prep_kernel,
        grid=(B, N // TOK),
        in_specs=[
            tok_spec(C_RK_R), tok_spec(C_RK_K), tok_spec(C_RK_V),
            pl.BlockSpec((None, TOK, LORA_PAD), lambda b, i: (b, i, 0)),
            const2((2, LORA_PAD, BRANCH_W)), const2((2, LORA_PAD, BRANCH_W)),
            const2((2, LORA_PAD, BRANCH_W)), const2((2, LORA_PAD, BRANCH_W)),
            const2((2, BRANCH_W)), const2((2, BRANCH_W)),
            const2((1, BRANCH_W)), const2((1, BRANCH_W)), const2((1, BRANCH_W)),
            const2((BRANCH_W, BRANCH_W)),
        ],
        out_specs=[o1, o1, o1, o2, o2, o2, o1],
        out_shape=[nat, nat, nat, nat2, nat2, nat2, nat],
        compiler_params=_cparams(("arbitrary", "arbitrary")),
        name="rwkv_prep",
    )(proj, proj, proj, lora, wup_hi, wup_lo, aup_hi, aup_lo, w0, a0,
      k_k.reshape(1, -1), k_a.reshape(1, -1), r_k.reshape(1, -1), ones)


SCAN_T = 32
SCAN_V = RWKV_HEAD // 2
N_KEYED = 5


def _rwkv_scan_kernel(*refs):
    fwd = refs[0:N_KEYED]
    mir = refs[N_KEYED:2 * N_KEYED]
    kkn_f, kkn_m, v_f, v_m, of_ref, om_ref = refs[2 * N_KEYED:2 * N_KEYED + 6]
    cw, ckka, ckd, cr, ckk, cv, s_ref, sa_ref = refs[2 * N_KEYED + 6:]
    comb = (cw, ckka, ckd, cr, ckk)

    @pl.when(pl.program_id(0) == 0)
    def _():
        s_ref[...] = jnp.zeros_like(s_ref)
        sa_ref[...] = jnp.zeros_like(sa_ref)

    def backward_lanes(rows):
        return (lax.broadcasted_iota(jnp.int32, (rows, LANES), 1) // (LANES // 4)) % 2 == 1

    backward = backward_lanes(RWKV_HEAD)
    backward_v = backward_lanes(SCAN_V)

    def merge(t, carry):
        for f, m, c_ref in zip(fwd, mir, comb):
            c_ref[t] = jnp.where(backward, m[SCAN_T - 1 - t], f[t])
        cv[t] = jnp.where(backward_v, v_m[SCAN_T - 1 - t], v_f[t])
        return carry

    lax.fori_loop(0, SCAN_T, merge, 0)
    ckk[SCAN_T] = jnp.where(backward, kkn_m[0], kkn_f[0])

    def step(t, sa):
        v_t = cv[t]
        o = jnp.zeros((SCAN_V, LANES), F32)
        nsa = jnp.zeros((SCAN_V, LANES), F32)
        for k in range(RWKV_HEAD):
            row = pl.ds(k, 1)
            s = s_ref[k] * cw[t, row, :] + sa * ckka[t, row, :] + v_t * ckd[t, row, :]
            s_ref[k] = s
            o = o + s * cr[t, row, :]
            nsa = nsa + s * ckk[t + 1, row, :]
        of_ref[t] = o
        om_ref[SCAN_T - 1 - t] = o
        return -nsa

    sa_ref[...] = lax.fori_loop(0, SCAN_T, step, sa_ref[...])


def _rwkv_scan(keyed, v, n_ctx):
    N = v.shape[0]
    nb, nc = N // SCAN_T, n_ctx // SCAN_T

    def mirror(i):
        return jnp.where(i < nc, nc - 1 - i, nb + nc - 1 - i)

    def next_f(i):
        return jnp.minimum((i + 1) * SCAN_T, N - 1)

    def next_m(i):
        j = jnp.minimum(i + 1, nb - 1)
        return mirror(j) * SCAN_T + SCAN_T - 1

    kf = pl.BlockSpec((SCAN_T, RWKV_HEAD, LANES), lambda i: (i, 0, 0))
    km = pl.BlockSpec((SCAN_T, RWKV_HEAD, LANES), lambda i: (mirror(i), 0, 0))
    vf = pl.BlockSpec((SCAN_T, SCAN_V, LANES), lambda i: (i, 0, 0))
    vm = pl.BlockSpec((SCAN_T, SCAN_V, LANES), lambda i: (mirror(i), 0, 0))
    kk = keyed[-1]
    out = jax.ShapeDtypeStruct((N, SCAN_V, LANES), F32)
    key_tile = lambda n: pltpu.VMEM((n, RWKV_HEAD, LANES), F32)
    return pl.pallas_call(
        _rwkv_scan_kernel,
        grid=(nb,),
        in_specs=[kf] * N_KEYED + [km] * N_KEYED + [
            pl.BlockSpec((1, RWKV_HEAD, LANES), lambda i: (next_f(i), 0, 0)),
            pl.BlockSpec((1, RWKV_HEAD, LANES), lambda i: (next_m(i), 0, 0)),
            vf, vm],
        out_specs=[vf, vm],
        out_shape=[out, out],
        scratch_shapes=[key_tile(SCAN_T)] * (N_KEYED - 1) + [key_tile(SCAN_T + 1),
                        pltpu.VMEM((SCAN_T, SCAN_V, LANES), F32),
                        pltpu.VMEM((RWKV_HEAD, SCAN_V, LANES), F32), pltpu.VMEM((SCAN_V, LANES), F32)],
        compiler_params=_cparams(("arbitrary",)),
        name="rwkv_scan",
    )(*keyed, *keyed, kk, kk, v, v)


def _rwkv_post_kernel(y0_ref, y1_ref, bonus_ref, gate_ref, g_ref, b_ref, ones_ref, o_ref):
    y = y0_ref[...] + y1_ref[...]
    ones = ones_ref[...]
    mu = _split_dot(y, ones) * (1.0 / RWKV_HEAD)
    d = y - mu
    var = _split_dot(d * d, ones) * (1.0 / RWKV_HEAD)
    yn = d * lax.rsqrt(var + RWKV_GN_EPS) * g_ref[...] + b_ref[...]
    o_ref[...] = ((yn + bonus_ref[...]) * _silu(gate_ref[...].astype(F32))).astype(o_ref.dtype)


def _rwkv_post(y0, y1, bonus, proj, gn_g, gn_b, ones):
    B, N, _ = y0.shape
    nat = pl.BlockSpec((None, TOK, BRANCH_W), lambda b, i: (b, i, 0))
    const2 = lambda shape: pl.BlockSpec(shape, lambda b, i: (0, 0))
    return pl.pallas_call(
        _rwkv_post_kernel,
        grid=(B, N // TOK),
        in_specs=[nat, nat, nat,
                  pl.BlockSpec((None, TOK, BRANCH_W), lambda b, i: (b, i, C_RK_G // BRANCH_W)),
                  const2((1, BRANCH_W)), const2((1, BRANCH_W)), const2((BRANCH_W, BRANCH_W))],
        out_specs=nat,
        out_shape=jax.ShapeDtypeStruct((B, N, BRANCH_W), BF16),
        compiler_params=_cparams(("arbitrary", "arbitrary")),
        name="rwkv_post",
    )(y0, y1, bonus, proj, gn_g.reshape(1, -1), gn_b.reshape(1, -1), ones)


def _to_scan_keys(t):
    if t.ndim == 3:
        t = jnp.broadcast_to(t[None], (2,) + t.shape)
    _, B, N, _ = t.shape
    t = t.reshape(2, B, N, RWKV_HEADS, RWKV_HEAD).transpose(2, 4, 0, 1, 3).reshape(N, RWKV_HEAD, 2 * B * RWKV_HEADS)
    return jnp.concatenate([t, t], axis=-1)


def _to_scan_values(t):
    B, N, _ = t.shape
    t = jnp.broadcast_to(t[None], (2,) + t.shape)
    t = t.reshape(2, B, N, RWKV_HEADS, 2, SCAN_V).transpose(2, 5, 4, 0, 1, 3)
    return t.reshape(N, SCAN_V, 2 * 2 * B * RWKV_HEADS)


def _from_scan_values(o_f, o_m, B):
    N = o_f.shape[0]
    lane = lax.broadcasted_iota(jnp.int32, (1, 1, LANES), 2)
    o = jnp.where((lane // (LANES // 4)) % 2 == 1, o_m, o_f)
    o = o.reshape(N, SCAN_V, 2, 2, B, RWKV_HEADS).transpose(3, 4, 0, 5, 2, 1).reshape(2, B, N, BRANCH_W)
    return o[0], o[1]


def _rwkv_branch(proj, lora, li, p, ones, n_ctx):
    B, N, _ = proj.shape
    assert 2 * 2 * B * RWKV_HEADS == LANES, "scan layout packs (value half, direction, sample, head) on the lanes"
    assert n_ctx % SCAN_T == 0 and N % SCAN_T == 0
    r, kk, v, w, kka, kd, bonus = _rwkv_prep(
        proj, lora, p["wup"][li], p["aup"][li], p["rwkv_w0"][li], p["rwkv_a0"][li],
        p["rwkv_k_k"][li], p["rwkv_k_a"][li], p["rwkv_r_k"][li], ones)
    keyed = [_to_scan_keys(t) for t in (w, kka, kd, r, kk)]
    o_f, o_m = _rwkv_scan(keyed, _to_scan_values(v), n_ctx)
    y0, y1 = _from_scan_values(o_f, o_m, B)
    return _rwkv_post(y0, y1, bonus, proj, p["rwkv_gn_g"][li], p["rwkv_gn_b"][li], ones)


def _merge_kernel(bc_ref, br_ref, bd_ref, bq_ref, lg_ref, wb_ref, bg_ref, wo_ref, x_ref, gp_ref, gt_ref, o_ref):
    merged = jnp.zeros((TOK, D_MODEL), F32)
    for j, b_ref in enumerate((bc_ref, br_ref, bd_ref, bq_ref)):
        pj = jnp.dot(b_ref[...], wb_ref[j], preferred_element_type=F32)
        gj = jax.nn.sigmoid(lg_ref[:, j * D_MODEL:(j + 1) * D_MODEL].astype(F32) + bg_ref[j:j + 1, :])
        merged = merged + pj * gj
    y = jnp.dot(merged.astype(BF16), wo_ref[...], preferred_element_type=F32)
    y = y * lax.rsqrt(jnp.mean(y * y, axis=-1, keepdims=True) + NORM_EPS) * gp_ref[...]
    o_ref[...] = x_ref[...] + gt_ref[...] * y


def _merge(branches, proj, w_branch, b_gate, w_out, li, xa, g_post, mod_l):
    B, N, _ = xa.shape
    row = _mod_row(B)
    bspec = pl.BlockSpec((None, TOK, BRANCH_W), lambda b, i: (b, i, 0))
    xspec = pl.BlockSpec((None, TOK, D_MODEL), lambda b, i: (b, i, 0))
    once = pl.Buffered(1)
    return pl.pallas_call(
        _merge_kernel,
        grid=(B, N // TOK),
        in_specs=[bspec] * 4 + [
            pl.BlockSpec((None, TOK, N_BRANCH * D_MODEL), lambda b, i: (b, i, C_MERGE // (N_BRANCH * D_MODEL))),
            pl.BlockSpec((None, N_BRANCH, BRANCH_W, D_MODEL), lambda b, i: (li, 0, 0, 0), pipeline_mode=once),
            pl.BlockSpec((None, N_BRANCH, D_MODEL), lambda b, i: (li, 0, 0)),
            pl.BlockSpec((None, D_MODEL, D_MODEL), lambda b, i: (li, 0, 0), pipeline_mode=once),
            xspec,
            pl.BlockSpec((1, D_MODEL), lambda b, i: (0, 0)),
            pl.BlockSpec((None, 1, D_MODEL), lambda b, i: row(b, i)[:2] + (2,)),
        ],
        out_specs=xspec,
        out_shape=jax.ShapeDtypeStruct((B, N, D_MODEL), F32),
        compiler_params=_cparams(("arbitrary", "arbitrary")),
        name="merge",
    )(*branches, proj, w_branch, b_gate, w_out, xa, g_post.reshape(1, -1), mod_l)


def kernel(x, c, ctx, c_ctx, norm_pre_g, norm_post_g, w_mod, b_mod, w_in, conv_w, conv_b, conv_ln_g, conv_ln_b,
           rwkv_w0, rwkv_w_up, rwkv_a0, rwkv_a_up, rwkv_k_k, rwkv_k_a, rwkv_r_k, rwkv_gn_g, rwkv_gn_b,
           diff_lam, diff_subln_g, gqa_qk_norm_g, w_branch, b_gate, w_out):
    B, n_lat, _ = x.shape
    n_ctx = ctx.shape[1]
    depth = w_in.shape[0]
    N = n_ctx + n_lat
    assert n_ctx == TOK and n_lat % TOK == 0 and B + 1 <= MOD_ROWS

    pad = jnp.zeros(w_in.shape[:2] + (LORA_PAD - DECAY_LORA - AAA_LORA,), w_in.dtype)
    w_in_r = jnp.concatenate(
        [w_in[..., O_MERGE:], w_in[..., :O_LORA], w_in[..., O_RK_G:O_GQ_K], w_in[..., O_GQ_G:O_MERGE],
         w_in[..., O_GQ_K:O_GQ_G], w_in[..., O_LORA:O_RK_G], pad], axis=-1).astype(BF16)
    assert w_in_r.shape[-1] == D_PROJ
    w_branch_b = w_branch.astype(BF16)
    w_out_b = w_out.astype(BF16)
    zrows = lambda t, lo: jnp.pad(t, ((0, 0), (0, 0), (lo, LORA_PAD - lo - t.shape[2]), (0, 0)))
    params = dict(wup=zrows(rwkv_w_up, 0), aup=zrows(rwkv_a_up, DECAY_LORA), rwkv_w0=rwkv_w0, rwkv_a0=rwkv_a0,
                  rwkv_k_k=rwkv_k_k, rwkv_k_a=rwkv_k_a, rwkv_r_k=rwkv_r_k.reshape(depth, BRANCH_W),
                  rwkv_gn_g=rwkv_gn_g, rwkv_gn_b=rwkv_gn_b)
    head = np.arange(BRANCH_W) // RWKV_HEAD
    ones = jnp.asarray(head[:, None] == head[None, :], BF16)
    cos_d, sin_d = _rope_tables(n_ctx, n_lat, DIFF_HALF, 2)
    cos_g, sin_g = _rope_tables(n_ctx, n_lat, GQA_HEAD, 1)

    cvec = jnp.concatenate([c, c_ctx[None, :], jnp.zeros((MOD_ROWS - B - 1, D_MODEL), c.dtype)], axis=0)
    mod = _modulation(cvec, w_mod, b_mod)
    xa = jnp.concatenate([ctx, x], axis=1)

    for li in range(depth):
        h = _prenorm(xa, mod[li], norm_pre_g[li]).reshape(B * N, D_MODEL)
        tm = 1088 if (B * N) % 1088 == 0 else TOK
        proj = _project(h, w_in_r, li, 0, C_LORA, 512, BF16, tm)
        lora = _project(h, w_in_r, li, C_LORA, LORA_PAD, LORA_PAD, F32, tm)
        proj = proj.reshape(B, N, C_LORA)
        lora = lora.reshape(B, N, LORA_PAD)
        lam_init = 0.8 - 0.6 * math.exp(-0.3 * li)
        br_conv = _conv_branch(proj, conv_w[li], conv_b[li], conv_ln_g[li], conv_ln_b[li])
        br_rwkv = _rwkv_branch(proj, lora, li, params, ones, n_ctx)
        br_diff = _diff_branch(proj, diff_lam[li], diff_subln_g[li], cos_d, sin_d, n_ctx, lam_init)
        br_gqa = _gqa_branch(proj, gqa_qk_norm_g[li], cos_g, sin_g, n_ctx)
        xa = _merge((br_conv, br_rwkv, br_diff, br_gqa), proj, w_branch_b, b_gate, w_out_b, li, xa,
                    norm_post_g[li], mod[li])
    return xa[:, n_ctx:]
```

```python
import functools
import math

import jax
import jax.numpy as jnp
import numpy as np
from jax import lax
from jax.experimental import pallas as pl
from jax.experimental.pallas import tpu as pltpu

F32 = jnp.float32
BF16 = jnp.bfloat16

D_MODEL = 2048
BRANCH_W = D_MODEL // 2
N_BRANCH = 4
GRID_W = 64
CONV_K = 31
CONV_HALO = 16
RWKV_HEAD = 64
RWKV_HEADS = BRANCH_W // RWKV_HEAD
DECAY_LORA = 96
AAA_LORA = 96
LORA_PAD = 256
RWKV_GN_EPS = 64e-5
DIFF_HALF = 64
DIFF_VDIM = 2 * DIFF_HALF
DIFF_HEADS = BRANCH_W // DIFF_VDIM
GQA_HEAD = 128
GQA_HEADS = BRANCH_W // GQA_HEAD
GQA_KV_HEADS = GQA_HEADS // 4
GQA_REP = GQA_HEADS // GQA_KV_HEADS
ROPE_THETA = 10000.0
NORM_EPS = 1e-6

TOK = 256
LANES = 128
SUBLANES = 8
MOD_ROWS = 8
VMEM_LIMIT = 60 * 1024 * 1024

O_A = 0
O_LORA = 6 * BRANCH_W
O_B = O_LORA + DECAY_LORA + AAA_LORA
O_C = O_B + 6 * BRANCH_W + 2 * GQA_KV_HEADS * GQA_HEAD + BRANCH_W
D_IN = O_C + N_BRANCH * D_MODEL
A_CONV, A_RK_R, A_RK_K, A_RK_V = 0, 3 * BRANCH_W, 4 * BRANCH_W, 5 * BRANCH_W
B_RK_G, B_DF_Q, B_DF_K, B_DF_V, B_DF_G, B_GQ_Q = (j * BRANCH_W for j in range(6))
B_GQ_K = 6 * BRANCH_W
B_GQ_V = B_GQ_K + GQA_KV_HEADS * GQA_HEAD
B_GQ_G = B_GQ_V + GQA_KV_HEADS * GQA_HEAD


def _cparams(sem):
    return pltpu.CompilerParams(dimension_semantics=sem, vmem_limit_bytes=VMEM_LIMIT)


def _silu(t):
    return t * jax.nn.sigmoid(t)


def _split_dot(a, b_hi, b_lo=None):
    a_hi = a.astype(BF16)
    a_lo = (a - a_hi.astype(F32)).astype(BF16)
    out = jnp.dot(a_hi, b_hi, preferred_element_type=F32) + jnp.dot(a_lo, b_hi, preferred_element_type=F32)
    if b_lo is not None:
        out = out + jnp.dot(a_hi, b_lo, preferred_element_type=F32)
    return out


def _mod_kernel(c_ref, w_ref, b_ref, o_ref):
    cv = c_ref[...]
    s = _silu(cv).astype(BF16)
    o_ref[...] = jnp.dot(s, w_ref[...].astype(BF16), preferred_element_type=F32) + b_ref[...]


def _modulation(cvec, w_mod, b_mod):
    L = w_mod.shape[0]
    tn = 1536
    out = pl.pallas_call(
        _mod_kernel,
        grid=(L, 3 * D_MODEL // tn),
        in_specs=[
            pl.BlockSpec((MOD_ROWS, D_MODEL), lambda l, j: (0, 0)),
            pl.BlockSpec((None, D_MODEL, tn), lambda l, j: (l, 0, j)),
            pl.BlockSpec((None, 1, tn), lambda l, j: (l, 0, j)),
        ],
        out_specs=pl.BlockSpec((None, MOD_ROWS, tn), lambda l, j: (l, 0, j)),
        out_shape=jax.ShapeDtypeStruct((L, MOD_ROWS, 3 * D_MODEL), F32),
        compiler_params=_cparams(("arbitrary", "arbitrary")),
        name="modulation",
    )(cvec, w_mod, b_mod.reshape(L, 1, 3 * D_MODEL))
    return out.reshape(L, MOD_ROWS, 1, 3 * D_MODEL)


def _mod_row(batch):
    return lambda b, i: (jnp.where(i == 0, batch, b), 0, 0)


def _prenorm_kernel(x_ref, sh_ref, sc_ref, g_ref, h_ref):
    x = x_ref[...]
    y = x * lax.rsqrt(jnp.mean(x * x, axis=-1, keepdims=True) + NORM_EPS) * g_ref[...]
    h_ref[...] = (y * (1.0 + sc_ref[...]) + sh_ref[...]).astype(h_ref.dtype)


def _prenorm(xa, mod_l, g):
    B, N, _ = xa.shape
    row = _mod_row(B)
    return pl.pallas_call(
        _prenorm_kernel,
        grid=(B, N // TOK),
        in_specs=[
            pl.BlockSpec((None, TOK, D_MODEL), lambda b, i: (b, i, 0)),
            pl.BlockSpec((None, 1, D_MODEL), lambda b, i: row(b, i)[:2] + (0,)),
            pl.BlockSpec((None, 1, D_MODEL), lambda b, i: row(b, i)[:2] + (1,)),
            pl.BlockSpec((1, D_MODEL), lambda b, i: (0, 0)),
        ],
        out_specs=pl.BlockSpec((None, TOK, D_MODEL), lambda b, i: (b, i, 0)),
        out_shape=jax.ShapeDtypeStruct((B, N, D_MODEL), BF16),
        compiler_params=_cparams(("arbitrary", "arbitrary")),
        name="prenorm",
    )(xa, mod_l, mod_l, g.reshape(1, D_MODEL))


def _mm_kernel(a_ref, w_ref, o_ref):
    o_ref[...] = jnp.dot(a_ref[...], w_ref[...], preferred_element_type=F32).astype(o_ref.dtype)


def _project(h, w_all, li, tn, out_dtype, tm, shape3):
    M, K = h.shape
    ncols = w_all.shape[-1]
    out = pl.pallas_call(
        _mm_kernel,
        grid=(ncols // tn, M // tm),
        in_specs=[
            pl.BlockSpec((tm, K), lambda j, i: (i, 0)),
            pl.BlockSpec((None, K, tn), lambda j, i: (li, 0, j)),
        ],
        out_specs=pl.BlockSpec((tm, tn), lambda j, i: (i, j)),
        out_shape=jax.ShapeDtypeStruct((M, ncols), out_dtype),
        compiler_params=_cparams(("arbitrary", "arbitrary")),
        name="in_proj",
    )(h, w_all)
    return out.reshape(shape3)


def _conv_kernel(cur_ref, gate_ref, prev_ref, next_ref, cw_ref, cb_ref, lg_ref, lb_ref, o_ref, buf_ref, *, n_tiles):
    i = pl.program_id(1)

    def glu(t):
        return t[:, :BRANCH_W].astype(F32) * jax.nn.sigmoid(t[:, BRANCH_W:2 * BRANCH_W].astype(F32))

    has_prev = jnp.logical_and(i != 0, i != 1)
    has_next = jnp.logical_and(i != 0, i != n_tiles - 1)
    cur = cur_ref[...]
    buf_ref[0:CONV_HALO, :] = jnp.where(has_prev, glu(prev_ref[...]), 0.0)
    buf_ref[CONV_HALO:CONV_HALO + TOK, :] = glu(cur)
    buf_ref[CONV_HALO + TOK:, :] = jnp.where(has_next, glu(next_ref[...]), 0.0)
    acc = jnp.broadcast_to(cb_ref[...], (TOK, BRANCH_W))
    base = CONV_HALO - CONV_K // 2
    for j in range(CONV_K):
        acc = acc + cw_ref[j:j + 1, :] * buf_ref[base + j:base + j + TOK, :]
    mu = jnp.mean(acc, axis=-1, keepdims=True)
    d = acc - mu
    var = jnp.mean(d * d, axis=-1, keepdims=True)
    y = d * lax.rsqrt(var + 1e-5) * lg_ref[...] + lb_ref[...]
    o_ref[...] = (_silu(y) * _silu(gate_ref[...].astype(F32))).astype(o_ref.dtype)


def _conv_branch(pa, conv_w, conv_b, ln_g, ln_b):
    B, N, _ = pa.shape
    nt = N // TOK
    hpt = TOK // CONV_HALO
    last = N // CONV_HALO - 1
    cblk = A_CONV // (2 * BRANCH_W)
    return pl.pallas_call(
        functools.partial(_conv_kernel, n_tiles=nt),
        grid=(B, nt),
        in_specs=[
            pl.BlockSpec((None, TOK, 2 * BRANCH_W), lambda b, i: (b, i, cblk)),
            pl.BlockSpec((None, TOK, BRANCH_W), lambda b, i: (b, i, 2 * cblk + 2)),
            pl.BlockSpec((None, CONV_HALO, 2 * BRANCH_W), lambda b, i: (b, jnp.maximum(i * hpt - 1, 0), cblk)),
            pl.BlockSpec((None, CONV_HALO, 2 * BRANCH_W), lambda b, i: (b, jnp.minimum((i + 1) * hpt, last), cblk)),
            pl.BlockSpec((CONV_K, BRANCH_W), lambda b, i: (0, 0)),
            pl.BlockSpec((1, BRANCH_W), lambda b, i: (0, 0)),
            pl.BlockSpec((1, BRANCH_W), lambda b, i: (0, 0)),
            pl.BlockSpec((1, BRANCH_W), lambda b, i: (0, 0)),
        ],
        out_specs=pl.BlockSpec((None, TOK, BRANCH_W), lambda b, i: (b, i, 0)),
        out_shape=jax.ShapeDtypeStruct((B, N, BRANCH_W), BF16),
        scratch_shapes=[pltpu.VMEM((TOK + 2 * CONV_HALO, BRANCH_W), F32)],
        compiler_params=_cparams(("arbitrary", "arbitrary")),
        name="conv_branch",
    )(pa, pa, pa, pa, conv_w, conv_b.reshape(1, -1), ln_g.reshape(1, -1), ln_b.reshape(1, -1))


def _rope_tables(n_ctx, n_lat, dim, reps):
    half = dim // 2
    t = np.arange(n_lat)
    inv = ROPE_THETA ** (-np.arange(0, half, 2, dtype=np.float64) / half)
    ang = np.concatenate([(t // GRID_W)[:, None] * inv, (t % GRID_W)[:, None] * inv], axis=-1)
    ang = np.concatenate([np.zeros((n_ctx, half)), ang], axis=0)
    cos = np.repeat(np.cos(ang), 2, axis=-1)
    sin = np.repeat(np.sin(ang), 2, axis=-1) * np.tile(np.array([-1.0, 1.0]), half)
    return (jnp.asarray(np.tile(cos, (1, reps)), F32), jnp.asarray(np.tile(sin, (1, reps)), F32))


def _rope(t, cos, sin_signed):
    lane = lax.broadcasted_iota(jnp.int32, t.shape, 1)
    partner = jnp.where(lane % 2 == 0, pltpu.roll(t, LANES - 1, 1), pltpu.roll(t, 1, 1))
    return t * cos + partner * sin_signed


LOG2_E = 1.4426950408889634


ATTN_CHUNK = 1536


def _scores_t(k, q):
    return lax.dot_general(k, q, (((1,), (1,)), ((), ())), preferred_element_type=F32)


def _online_attention(kp_ref, vt_ref, qs, nk):
    chunks = [(lo, min(ATTN_CHUNK, nk - lo)) for lo in range(0, nk, ATTN_CHUNK)]

    def scores(c):
        lo, n = chunks[c]
        k = kp_ref[lo:lo + n, :]
        return [_scores_t(k, q) for q in qs]

    state = [None] * len(qs)
    s_next = scores(0)
    for c, (lo, n) in enumerate(chunks):
        s_cur = s_next
        if c + 1 < len(chunks):
            s_next = scores(c + 1)
        v = vt_ref[:, lo:lo + n]
        for j, s in enumerate(s_cur):
            m = jnp.max(s, axis=0, keepdims=True)
            if state[j] is not None:
                m_old, l_old, acc_old = state[j]
                m = jnp.maximum(m_old, m)
            e = jnp.exp2(s - m)
            l = jnp.sum(e, axis=0, keepdims=True)
            acc = jnp.dot(v, e.astype(BF16), preferred_element_type=F32)
            if state[j] is not None:
                alpha = jnp.exp2(m_old - m)
                l = l_old * alpha + l
                acc = acc_old * alpha + acc
            state[j] = (m, l, acc)
    return [(acc, l) for _, l, acc in state]


def _gqa_kernel(q_ref, k_ref, v_ref, gate_ref, cos_ref, sin_ref, g_ref, o_ref, kp_ref, vt_ref, *, n_ctx, n_all):
    r = pl.program_id(2)
    qi = pl.program_id(3)

    @pl.when(jnp.logical_and(r == 0, qi == 0))
    def _():
        k = k_ref[...].astype(F32)
        k = k * lax.rsqrt(jnp.mean(k * k, axis=-1, keepdims=True) + NORM_EPS) * g_ref[1:2, :]
        kp_ref[...] = _rope(k, cos_ref[...], sin_ref[...]).astype(BF16)
        vt_ref[...] = v_ref[...].astype(F32).T.astype(BF16)

    rows = pl.ds(pl.multiple_of(qi * TOK, TOK), TOK)
    qs = []
    for j in range(GQA_PAIR):
        q = q_ref[:, j * GQA_HEAD:(j + 1) * GQA_HEAD].astype(F32)
        q = q * lax.rsqrt(jnp.mean(q * q, axis=-1, keepdims=True) + NORM_EPS) * g_ref[0:1, :]
        q = _rope(q, cos_ref[rows, :], sin_ref[rows, :]) * (GQA_HEAD ** -0.5 * LOG2_E)
        qs.append(q.astype(BF16))
    gate = _silu(gate_ref[...].astype(F32))

    def attend(nk):
        o = jnp.concatenate([(acc / l).T for acc, l in _online_attention(kp_ref, vt_ref, qs, nk)], axis=1)
        o_ref[...] = (o * gate).astype(o_ref.dtype)

    pl.when(qi == 0)(lambda: attend(n_ctx))
    pl.when(qi != 0)(lambda: attend(n_all))


GQA_PAIR = 2


def _gqa_branch(pb, qk_g, cos, sin, n_ctx):
    B, N, _ = pb.shape
    blk = lambda c0: c0 // GQA_HEAD
    wide = GQA_PAIR * GQA_HEAD
    steps = GQA_REP // GQA_PAIR
    return pl.pallas_call(
        functools.partial(_gqa_kernel, n_ctx=n_ctx, n_all=N),
        grid=(B, GQA_KV_HEADS, steps, N // TOK),
        in_specs=[
            pl.BlockSpec((None, TOK, wide), lambda b, g, r, i: (b, i, B_GQ_Q // wide + g * steps + r)),
            pl.BlockSpec((None, N, GQA_HEAD), lambda b, g, r, i: (b, 0, blk(B_GQ_K) + g)),
            pl.BlockSpec((None, N, GQA_HEAD), lambda b, g, r, i: (b, 0, blk(B_GQ_V) + g)),
            pl.BlockSpec((None, TOK, wide), lambda b, g, r, i: (b, i, B_GQ_G // wide + g * steps + r)),
            pl.BlockSpec((N, GQA_HEAD), lambda b, g, r, i: (0, 0)),
            pl.BlockSpec((N, GQA_HEAD), lambda b, g, r, i: (0, 0)),
            pl.BlockSpec((2, GQA_HEAD), lambda b, g, r, i: (0, 0)),
        ],
        out_specs=pl.BlockSpec((None, TOK, wide), lambda b, g, r, i: (b, i, g * steps + r)),
        out_shape=jax.ShapeDtypeStruct((B, N, BRANCH_W), BF16),
        scratch_shapes=[pltpu.VMEM((N, GQA_HEAD), BF16), pltpu.VMEM((GQA_HEAD, N), BF16)],
        compiler_params=_cparams(("arbitrary",) * 4),
        name="gqa_branch",
    )(pb, pb, pb, pb, cos, sin, qk_g)


def _diff_kernel(q_ref, k_ref, v_ref, gate_ref, cos_ref, sin_ref, lam_ref, sg_ref, o_ref, kp_ref, vt_ref,
                 *, n_ctx, n_all, lam_init):
    qi = pl.program_id(2)

    @pl.when(qi == 0)
    def _():
        kp_ref[...] = _rope(k_ref[...].astype(F32), cos_ref[...], sin_ref[...]).astype(BF16)
        vt_ref[...] = v_ref[...].astype(F32).T.astype(BF16)

    lp = lam_ref[...]
    lam = (jnp.exp(jnp.sum(lp[0:1] * lp[1:2], axis=-1, keepdims=True))
           - jnp.exp(jnp.sum(lp[2:3] * lp[3:4], axis=-1, keepdims=True)) + lam_init)
    rows = pl.ds(pl.multiple_of(qi * TOK, TOK), TOK)
    q = _rope(q_ref[...].astype(F32), cos_ref[rows, :], sin_ref[rows, :]) * (DIFF_HALF ** -0.5 * LOG2_E)
    lane = lax.broadcasted_iota(jnp.int32, q.shape, 1)
    q1 = jnp.where(lane < DIFF_HALF, q, 0.0).astype(BF16)
    q2 = jnp.where(lane >= DIFF_HALF, q, 0.0).astype(BF16)
    gate = _silu(gate_ref[...].astype(F32))

    def attend(nk):
        (acc1, l1), (acc2, l2) = _online_attention(kp_ref, vt_ref, (q1, q2), nk)
        o = (acc1 / l1 - acc2 * (lam / l2)).T
        o = o * lax.rsqrt(jnp.mean(o * o, axis=-1, keepdims=True) + NORM_EPS) * sg_ref[...]
        o_ref[...] = (o * (1.0 - lam_init) * gate).astype(o_ref.dtype)

    pl.when(qi == 0)(lambda: attend(n_ctx))
    pl.when(qi != 0)(lambda: attend(n_all))


def _diff_branch(pb, lam_p, subln_g, cos, sin, n_ctx, lam_init):
    B, N, _ = pb.shape
    blk = lambda c0: c0 // DIFF_VDIM
    return pl.pallas_call(
        functools.partial(_diff_kernel, n_ctx=n_ctx, n_all=N, lam_init=lam_init),
        grid=(B, DIFF_HEADS, N // TOK),
        in_specs=[
            pl.BlockSpec((None, TOK, DIFF_VDIM), lambda b, h, i: (b, i, blk(B_DF_Q) + h)),
            pl.BlockSpec((None, N, DIFF_VDIM), lambda b, h, i: (b, 0, blk(B_DF_K) + h)),
            pl.BlockSpec((None, N, DIFF_VDIM), lambda b, h, i: (b, 0, blk(B_DF_V) + h)),
            pl.BlockSpec((None, TOK, DIFF_VDIM), lambda b, h, i: (b, i, blk(B_DF_G) + h)),
            pl.BlockSpec((N, DIFF_VDIM), lambda b, h, i: (0, 0)),
            pl.BlockSpec((N, DIFF_VDIM), lambda b, h, i: (0, 0)),
            pl.BlockSpec((4, DIFF_HALF), lambda b, h, i: (0, 0)),
            pl.BlockSpec((1, DIFF_VDIM), lambda b, h, i: (0, 0)),
        ],
        out_specs=pl.BlockSpec((None, TOK, DIFF_VDIM), lambda b, h, i: (b, i, h)),
        out_shape=jax.ShapeDtypeStruct((B, N, BRANCH_W), BF16),
        scratch_shapes=[pltpu.VMEM((N, DIFF_VDIM), BF16), pltpu.VMEM((DIFF_VDIM, N), BF16)],
        compiler_params=_cparams(("arbitrary",) * 3),
        name="diff_branch",
    )(pb, pb, pb, pb, cos, sin, lam_p, subln_g.reshape(1, DIFF_VDIM))


XF_T = 128
SCAN_T = 32
SCAN_Q = RWKV_HEAD // 4
KEYED = ("w0", "kka0", "kd0", "w1", "kka1", "kd1", "r", "kk")
KEY_GROUPS = 4


def _rwkv_prep_kernel(r_ref, k_ref, v_ref, lora_ref, wup_hi_ref, wup_lo_ref, aup_hi_ref, aup_lo_ref, w0_ref, a0_ref,
                      kk_g_ref, ka_ref, rk_ref, ones_ref, *refs):
    out = dict(zip(KEYED, refs[:len(KEYED)]))
    zv_ref, bonus_ref, y_ref = refs[len(KEYED):]
    n_b = r_ref.shape[0]
    slot = {name: j * n_b for j, name in enumerate(KEYED + ("v",))}
    copies = LANES // (n_b * RWKV_HEADS)
    group = pl.program_id(1)

    @pl.when(group == 0)
    def _():
        _rwkv_prep_tile(r_ref, k_ref, v_ref, lora_ref, wup_hi_ref, wup_lo_ref, aup_hi_ref, aup_lo_ref, w0_ref, a0_ref,
                        kk_g_ref, ka_ref, rk_ref, ones_ref, bonus_ref, y_ref, slot)
        for val in range(SCAN_Q):
            rows = [y_ref[slot["v"] + b, pl.ds(q * SCAN_Q + val, RWKV_HEADS, stride=RWKV_HEAD), :]
                    for q in range(copies) for b in range(n_b)]
            zv_ref[:, val, :] = jnp.concatenate(rows, axis=0).T

    for g in range(KEY_GROUPS):
        @pl.when(group == g)
        def _(g=g):
            for name in KEYED:
                for j in range(RWKV_HEAD // KEY_GROUPS):
                    key = g * (RWKV_HEAD // KEY_GROUPS) + j
                    rows = [y_ref[slot[name] + b, pl.ds(key, RWKV_HEADS, stride=RWKV_HEAD), :] for b in range(n_b)]
                    out[name][j] = jnp.concatenate(rows * copies, axis=0).T


def _rwkv_prep_tile(r_ref, k_ref, v_ref, lora_ref, wup_hi_ref, wup_lo_ref, aup_hi_ref, aup_lo_ref, w0_ref, a0_ref,
                    kk_g_ref, ka_ref, rk_ref, ones_ref, bonus_ref, y_ref, slot):
    n_b = r_ref.shape[0]
    ones = ones_ref[...]
    for b in range(n_b):
        r = r_ref[b].astype(F32)
        k = k_ref[b].astype(F32)
        v = v_ref[b].astype(F32)
        lora = lora_ref[b]
        wl = jnp.tanh(lora)
        kk = k * kk_g_ref[...]
        kk = kk * lax.rsqrt(jnp.maximum(_split_dot(kk * kk, ones), 1e-24))
        y_ref[slot["r"] + b] = r.T
        y_ref[slot["kk"] + b] = kk.T
        y_ref[slot["v"] + b] = v.T
        bsum = jnp.zeros_like(r)
        for e in range(2):
            w_pre = w0_ref[e:e + 1, :] + _split_dot(wl, wup_hi_ref[e], wup_lo_ref[e])
            y_ref[slot[f"w{e}"] + b] = jnp.exp(-jax.nn.sigmoid(w_pre) * math.exp(-0.5)).T
            a = jax.nn.sigmoid(a0_ref[e:e + 1, :] + _split_dot(lora, aup_hi_ref[e], aup_lo_ref[e]))
            kd = k * (1.0 + (a - 1.0) * ka_ref[...])
            y_ref[slot[f"kd{e}"] + b] = kd.T
            y_ref[slot[f"kka{e}"] + b] = (kk * a).T
            bsum = bsum + kd
        bonus_ref[b] = _split_dot(r * bsum * rk_ref[...], ones) * v


def _rwkv_prep(pa, lora, wup, aup, w0, a0, k_k, k_a, r_k, ones):
    B, N, _ = pa.shape
    blk = lambda c0: c0 // BRANCH_W
    tok_spec = lambda c0: pl.BlockSpec((B, XF_T, BRANCH_W), lambda i, g: (0, i, blk(c0)))
    const = lambda shape: pl.BlockSpec(shape, lambda i, g: (0,) * len(shape), pipeline_mode=pl.Buffered(1))
    split = lambda t: (t.astype(BF16), (t - t.astype(BF16).astype(F32)).astype(BF16))
    wup_hi, wup_lo = split(wup)
    aup_hi, aup_lo = split(aup)
    keyed_shape = jax.ShapeDtypeStruct((N // XF_T, RWKV_HEAD, XF_T, LANES), F32)
    keyed_spec = pl.BlockSpec((None, RWKV_HEAD // KEY_GROUPS, XF_T, LANES), lambda i, g: (i, g, 0, 0))
    return pl.pallas_call(
        _rwkv_prep_kernel,
        grid=(N // XF_T, KEY_GROUPS),
        in_specs=[
            tok_spec(A_RK_R), tok_spec(A_RK_K), tok_spec(A_RK_V),
            pl.BlockSpec((B, XF_T, LORA_PAD), lambda i, g: (0, i, 0)),
            const((2, LORA_PAD, BRANCH_W)), const((2, LORA_PAD, BRANCH_W)),
            const((2, LORA_PAD, BRANCH_W)), const((2, LORA_PAD, BRANCH_W)),
            const((2, BRANCH_W)), const((2, BRANCH_W)),
            const((1, BRANCH_W)), const((1, BRANCH_W)), const((1, BRANCH_W)),
            const((BRANCH_W, BRANCH_W)),
        ],
        out_specs=[keyed_spec] * len(KEYED) + [
            pl.BlockSpec((XF_T, SCAN_Q, LANES), lambda i, g: (i, 0, 0)),
            pl.BlockSpec((B, XF_T, BRANCH_W), lambda i, g: (0, i, 0))],
        out_shape=[keyed_shape] * len(KEYED) + [
            jax.ShapeDtypeStruct((N, SCAN_Q, LANES), F32),
            jax.ShapeDtypeStruct((B, N, BRANCH_W), F32)],
        scratch_shapes=[pltpu.VMEM(((len(KEYED) + 1) * B, BRANCH_W, XF_T), F32)],
        compiler_params=_cparams(("arbitrary", "arbitrary")),
        name="rwkv_prep",
    )(pa, pa, pa, lora, wup_hi, wup_lo, aup_hi, aup_lo, w0, a0,
      k_k.reshape(1, -1), k_a.reshape(1, -1), r_k.reshape(1, -1), ones)


def _rwkv_scan_kernel(wf, kkaf, kdf, wb, kkab, kdb, rf, kkf, rb, kkb, kkn_f, kkn_b, vf, vb,
                      of_ref, ob_ref, ckf, ckb, sf_ref, sb_ref, saf_ref, sab_ref):
    @pl.when(pl.program_id(0) == 0)
    def _():
        for ref in (sf_ref, sb_ref, saf_ref, sab_ref):
            ref[...] = jnp.zeros_like(ref)

    ckf[:, 0:SCAN_T, :] = kkf[...]
    ckf[:, SCAN_T:, :] = kkn_f[...]
    ckb[:, 0:SUBLANES, :] = kkn_b[...]
    ckb[:, SUBLANES:, :] = kkb[...]

    def chain(s_ref, sa, v_t, w, kka, kd, r, kk_next, t, t_next):
        o = jnp.zeros((SCAN_Q, LANES), F32)
        nsa = jnp.zeros((SCAN_Q, LANES), F32)
        row = pl.ds(t, 1)
        row_next = pl.ds(t_next, 1)
        for k in range(RWKV_HEAD):
            s = s_ref[k] * w[k, row, :] + sa * kka[k, row, :] + v_t * kd[k, row, :]
            s_ref[k] = s
            o = o + s * r[k, row, :]
            nsa = nsa + s * kk_next[k, row_next, :]
        return o, -nsa

    def step(t, carry):
        sa_f, sa_b = carry
        tb = SCAN_T - 1 - t
        o_f, sa_f = chain(sf_ref, sa_f, vf[t], wf, kkaf, kdf, rf, ckf, t, t + 1)
        o_b, sa_b = chain(sb_ref, sa_b, vb[tb], wb, kkab, kdb, rb, ckb, tb, tb + SUBLANES - 1)
        of_ref[t] = o_f
        ob_ref[tb] = o_b
        return sa_f, sa_b

    saf_ref[...], sab_ref[...] = lax.fori_loop(0, SCAN_T, step, (saf_ref[...], sab_ref[...]))


def _rwkv_scan(z, zv, n_ctx):
    N = zv.shape[0]
    nb, nc = N // SCAN_T, n_ctx // SCAN_T
    per_tile = XF_T // SCAN_T

    def mirror(i):
        return jnp.where(i < nc, nc - 1 - i, nb + nc - 1 - i)

    def tok_f(i):
        return jnp.minimum((i + 1) * SCAN_T, N - SUBLANES)

    def tok_b(i):
        return mirror(jnp.minimum(i + 1, nb - 1)) * SCAN_T + SCAN_T - 1

    def keyed(block_of):
        return pl.BlockSpec((None, RWKV_HEAD, SCAN_T, LANES),
                            lambda i: (block_of(i) // per_tile, 0, block_of(i) % per_tile, 0))

    def group(tok_of):
        return pl.BlockSpec((None, RWKV_HEAD, SUBLANES, LANES),
                            lambda i: (tok_of(i) // XF_T, 0, (tok_of(i) % XF_T) // SUBLANES, 0))

    fwd, bwd = keyed(lambda i: i), keyed(mirror)
    vf = pl.BlockSpec((SCAN_T, SCAN_Q, LANES), lambda i: (i, 0, 0))
    vb = pl.BlockSpec((SCAN_T, SCAN_Q, LANES), lambda i: (mirror(i), 0, 0))
    out = jax.ShapeDtypeStruct((N, SCAN_Q, LANES), F32)
    state = pltpu.VMEM((RWKV_HEAD, SCAN_Q, LANES), F32)
    return pl.pallas_call(
        _rwkv_scan_kernel,
        grid=(nb,),
        in_specs=[fwd] * 3 + [bwd] * 3 + [fwd, fwd, bwd, bwd, group(tok_f), group(tok_b), vf, vb],
        out_specs=[vf, vb],
        out_shape=[out, out],
        scratch_shapes=[pltpu.VMEM((RWKV_HEAD, SCAN_T + SUBLANES, LANES), F32)] * 2 + [state, state] + [
            pltpu.VMEM((SCAN_Q, LANES), F32)] * 2,
        compiler_params=_cparams(("arbitrary",)),
        name="rwkv_scan",
    )(z["w0"], z["kka0"], z["kd0"], z["w1"], z["kka1"], z["kd1"], z["r"], z["kk"], z["r"], z["kk"],
      z["kk"], z["kk"], zv, zv)


def _rwkv_post_kernel(of_ref, ob_ref, bonus_ref, gate_ref, g_ref, b_ref, ones_ref, o_ref, y_ref):
    n_b = bonus_ref.shape[0]
    copies = LANES // (n_b * RWKV_HEADS)
    for val in range(SCAN_Q):
        t = (of_ref[:, val, :] + ob_ref[:, val, :]).T
        for q in range(copies):
            for b in range(n_b):
                lo = (q * n_b + b) * RWKV_HEADS
                y_ref[b, pl.ds(q * SCAN_Q + val, RWKV_HEADS, stride=RWKV_HEAD), :] = t[lo:lo + RWKV_HEADS]
    ones = ones_ref[...]
    for b in range(n_b):
        y = y_ref[b].T
        mu = _split_dot(y, ones) * (1.0 / RWKV_HEAD)
        d = y - mu
        var = _split_dot(d * d, ones) * (1.0 / RWKV_HEAD)
        yn = d * lax.rsqrt(var + RWKV_GN_EPS) * g_ref[...] + b_ref[...]
        o_ref[b] = ((yn + bonus_ref[b]) * _silu(gate_ref[b].astype(F32))).astype(o_ref.dtype)


def _rwkv_post(o_f, o_b, bonus, pb, gn_g, gn_b, ones):
    B, N, _ = bonus.shape
    nat = pl.BlockSpec((B, XF_T, BRANCH_W), lambda i: (0, i, 0))
    scan = pl.BlockSpec((XF_T, SCAN_Q, LANES), lambda i: (i, 0, 0))
    const = lambda shape: pl.BlockSpec(shape, lambda i: (0, 0))
    return pl.pallas_call(
        _rwkv_post_kernel,
        grid=(N // XF_T,),
        in_specs=[scan, scan, nat,
                  pl.BlockSpec((B, XF_T, BRANCH_W), lambda i: (0, i, B_RK_G // BRANCH_W)),
                  const((1, BRANCH_W)), const((1, BRANCH_W)), const((BRANCH_W, BRANCH_W))],
        out_specs=nat,
        out_shape=jax.ShapeDtypeStruct((B, N, BRANCH_W), BF16),
        scratch_shapes=[pltpu.VMEM((B, BRANCH_W, XF_T), F32)],
        compiler_params=_cparams(("arbitrary",)),
        name="rwkv_post",
    )(o_f, o_b, bonus, pb, gn_g.reshape(1, -1), gn_b.reshape(1, -1), ones)


def _rwkv_branch(pa, pb, lora, li, p, ones, n_ctx):
    B, N, _ = pa.shape
    assert LANES % (B * RWKV_HEADS) == 0 and LANES // (B * RWKV_HEADS) * SCAN_Q == RWKV_HEAD
    assert n_ctx % SCAN_T == 0 and N % XF_T == 0 and XF_T % SCAN_T == 0
    outs = _rwkv_prep(pa, lora, p["wup"][li], p["aup"][li], p["rwkv_w0"][li], p["rwkv_a0"][li],
                      p["rwkv_k_k"][li], p["rwkv_k_a"][li], p["rwkv_r_k"][li], ones)
    z = dict(zip(KEYED, outs[:len(KEYED)]))
    zv, bonus = outs[len(KEYED):]
    o_f, o_b = _rwkv_scan(z, zv, n_ctx)
    return _rwkv_post(o_f, o_b, bonus, pb, p["rwkv_gn_g"][li], p["rwkv_gn_b"][li], ones)


def _merge_kernel(bc_ref, br_ref, bd_ref, bq_ref, lg_ref, wb_ref, bg_ref, wo_ref, x_ref, gp_ref, gt_ref, o_ref):
    merged = jnp.zeros((TOK, D_MODEL), F32)
    for j, b_ref in enumerate((bc_ref, br_ref, bd_ref, bq_ref)):
        pj = jnp.dot(b_ref[...], wb_ref[j], preferred_element_type=F32)
        gj = jax.nn.sigmoid(lg_ref[:, j * D_MODEL:(j + 1) * D_MODEL].astype(F32) + bg_ref[j:j + 1, :])
        merged = merged + pj * gj
    y = jnp.dot(merged.astype(BF16), wo_ref[...], preferred_element_type=F32)
    y = y * lax.rsqrt(jnp.mean(y * y, axis=-1, keepdims=True) + NORM_EPS) * gp_ref[...]
    o_ref[...] = x_ref[...] + gt_ref[...] * y


def _merge(branches, pc, w_branch, b_gate, w_out, li, xa, g_post, mod_l):
    B, N, _ = xa.shape
    row = _mod_row(B)
    bspec = pl.BlockSpec((None, TOK, BRANCH_W), lambda b, i: (b, i, 0))
    xspec = pl.BlockSpec((None, TOK, D_MODEL), lambda b, i: (b, i, 0))
    once = pl.Buffered(1)
    return pl.pallas_call(
        _merge_kernel,
        grid=(B, N // TOK),
        in_specs=[bspec] * 4 + [
            pl.BlockSpec((None, TOK, N_BRANCH * D_MODEL), lambda b, i: (b, i, 0)),
            pl.BlockSpec((None, N_BRANCH, BRANCH_W, D_MODEL), lambda b, i: (li, 0, 0, 0), pipeline_mode=once),
            pl.BlockSpec((None, N_BRANCH, D_MODEL), lambda b, i: (li, 0, 0)),
            pl.BlockSpec((None, D_MODEL, D_MODEL), lambda b, i: (li, 0, 0), pipeline_mode=once),
            xspec,
            pl.BlockSpec((1, D_MODEL), lambda b, i: (0, 0)),
            pl.BlockSpec((None, 1, D_MODEL), lambda b, i: row(b, i)[:2] + (2,)),
        ],
        out_specs=xspec,
        out_shape=jax.ShapeDtypeStruct((B, N, D_MODEL), F32),
        compiler_params=_cparams(("arbitrary", "arbitrary")),
        name="merge",
    )(*branches, pc, w_branch, b_gate, w_out, xa, g_post.reshape(1, -1), mod_l)


def kernel(x, c, ctx, c_ctx, norm_pre_g, norm_post_g, w_mod, b_mod, w_in, conv_w, conv_b, conv_ln_g, conv_ln_b,
           rwkv_w0, rwkv_w_up, rwkv_a0, rwkv_a_up, rwkv_k_k, rwkv_k_a, rwkv_r_k, rwkv_gn_g, rwkv_gn_b,
           diff_lam, diff_subln_g, gqa_qk_norm_g, w_branch, b_gate, w_out):
    B, n_lat, _ = x.shape
    n_ctx = ctx.shape[1]
    depth = w_in.shape[0]
    N = n_ctx + n_lat
    assert n_ctx == TOK and n_lat % TOK == 0 and B + 1 <= MOD_ROWS and w_in.shape[-1] == D_IN

    w_a = w_in[..., O_A:O_LORA].astype(BF16)
    w_b = w_in[..., O_B:O_C].astype(BF16)
    w_c = w_in[..., O_C:].astype(BF16)
    w_l = jnp.pad(w_in[..., O_LORA:O_B], ((0, 0), (0, 0), (0, LORA_PAD - DECAY_LORA - AAA_LORA))).astype(BF16)
    w_branch_b = w_branch.astype(BF16)
    w_out_b = w_out.astype(BF16)
    zrows = lambda t, lo: jnp.pad(t, ((0, 0), (0, 0), (lo, LORA_PAD - lo - t.shape[2]), (0, 0)))
    params = dict(wup=zrows(rwkv_w_up, 0), aup=zrows(rwkv_a_up, DECAY_LORA), rwkv_w0=rwkv_w0, rwkv_a0=rwkv_a0,
                  rwkv_k_k=rwkv_k_k, rwkv_k_a=rwkv_k_a, rwkv_r_k=rwkv_r_k.reshape(depth, BRANCH_W),
                  rwkv_gn_g=rwkv_gn_g, rwkv_gn_b=rwkv_gn_b)
    head = np.arange(BRANCH_W) // RWKV_HEAD
    ones = jnp.asarray(head[:, None] == head[None, :], BF16)
    cos_d, sin_d = _rope_tables(n_ctx, n_lat, DIFF_HALF, 2)
    cos_g, sin_g = _rope_tables(n_ctx, n_lat, GQA_HEAD, 1)

    cvec = jnp.concatenate([c, c_ctx[None, :], jnp.zeros((MOD_ROWS - B - 1, D_MODEL), c.dtype)], axis=0)
    mod = _modulation(cvec, w_mod, b_mod)
    xa = jnp.concatenate([ctx, x], axis=1)
    tm = 1088 if (B * N) % 1088 == 0 else TOK

    for li in range(depth):
        h = _prenorm(xa, mod[li], norm_pre_g[li]).reshape(B * N, D_MODEL)
        pa = _project(h, w_a, li, 512, BF16, tm, (B, N, O_LORA - O_A))
        pb = _project(h, w_b, li, 512, BF16, tm, (B, N, O_C - O_B))
        pc = _project(h, w_c, li, 512, BF16, tm, (B, N, D_IN - O_C))
        lora = _project(h, w_l, li, LORA_PAD, F32, tm, (B, N, LORA_PAD))
        lam_init = 0.8 - 0.6 * math.exp(-0.3 * li)
        br_conv = _conv_branch(pa, conv_w[li], conv_b[li], conv_ln_g[li], conv_ln_b[li])
        br_rwkv = _rwkv_branch(pa, pb, lora, li, params, ones, n_ctx)
        br_diff = _diff_branch(pb, diff_lam[li], diff_subln_g[li], cos_d, sin_d, n_ctx, lam_init)
        br_gqa = _gqa_branch(pb, gqa_qk_norm_g[li], cos_g, sin_g, n_ctx)
        xa = _merge((br_conv, br_rwkv, br_diff, br_gqa), pc, w_branch_b, b_gate, w_out_b, li, xa,
                    norm_post_g[li], mod[li])
    return xa[:, n_ctx:]
```

```python
import functools
import math

import jax
import jax.numpy as jnp
import numpy as np
from jax import lax
from jax.experimental import pallas as pl
from jax.experimental.pallas import tpu as pltpu

F32 = jnp.float32
BF16 = jnp.bfloat16

D_MODEL = 2048
BRANCH_W = D_MODEL // 2
N_BRANCH = 4
GRID_W = 64
CONV_K = 31
CONV_HALO = 16
RWKV_HEAD = 64
RWKV_HEADS = BRANCH_W // RWKV_HEAD
DECAY_LORA = 96
AAA_LORA = 96
LORA_PAD = 256
RWKV_GN_EPS = 64e-5
DIFF_HALF = 64
DIFF_VDIM = 2 * DIFF_HALF
DIFF_HEADS = BRANCH_W // DIFF_VDIM
GQA_HEAD = 128
GQA_HEADS = BRANCH_W // GQA_HEAD
GQA_KV_HEADS = GQA_HEADS // 4
GQA_REP = GQA_HEADS // GQA_KV_HEADS
ROPE_THETA = 10000.0
NORM_EPS = 1e-6

TOK = 256
LANES = 128
SUBLANES = 8
MOD_ROWS = 8
VMEM_LIMIT = 60 * 1024 * 1024

O_A = 0
O_LORA = 6 * BRANCH_W
O_B = O_LORA + DECAY_LORA + AAA_LORA
O_C = O_B + 6 * BRANCH_W + 2 * GQA_KV_HEADS * GQA_HEAD + BRANCH_W
D_IN = O_C + N_BRANCH * D_MODEL
A_CONV, A_RK_R, A_RK_K, A_RK_V = 0, 3 * BRANCH_W, 4 * BRANCH_W, 5 * BRANCH_W
B_RK_G, B_DF_Q, B_DF_K, B_DF_V, B_DF_G, B_GQ_Q = (j * BRANCH_W for j in range(6))
B_GQ_K = 6 * BRANCH_W
B_GQ_V = B_GQ_K + GQA_KV_HEADS * GQA_HEAD
B_GQ_G = B_GQ_V + GQA_KV_HEADS * GQA_HEAD


def _cparams(sem):
    return pltpu.CompilerParams(dimension_semantics=sem, vmem_limit_bytes=VMEM_LIMIT)


def _silu(t):
    return t * jax.nn.sigmoid(t)


def _split_dot(a, b_hi, b_lo=None):
    a_hi = a.astype(BF16)
    a_lo = (a - a_hi.astype(F32)).astype(BF16)
    out = jnp.dot(a_hi, b_hi, preferred_element_type=F32) + jnp.dot(a_lo, b_hi, preferred_element_type=F32)
    if b_lo is not None:
        out = out + jnp.dot(a_hi, b_lo, preferred_element_type=F32)
    return out


def _mod_kernel(c_ref, w_ref, b_ref, o_ref):
    cv = c_ref[...]
    s = _silu(cv).astype(BF16)
    o_ref[...] = jnp.dot(s, w_ref[...].astype(BF16), preferred_element_type=F32) + b_ref[...]


def _modulation(cvec, w_mod, b_mod):
    L = w_mod.shape[0]
    tn = 1536
    out = pl.pallas_call(
        _mod_kernel,
        grid=(L, 3 * D_MODEL // tn),
        in_specs=[
            pl.BlockSpec((MOD_ROWS, D_MODEL), lambda l, j: (0, 0)),
            pl.BlockSpec((None, D_MODEL, tn), lambda l, j: (l, 0, j)),
            pl.BlockSpec((None, 1, tn), lambda l, j: (l, 0, j)),
        ],
        out_specs=pl.BlockSpec((None, MOD_ROWS, tn), lambda l, j: (l, 0, j)),
        out_shape=jax.ShapeDtypeStruct((L, MOD_ROWS, 3 * D_MODEL), F32),
        compiler_params=_cparams(("arbitrary", "arbitrary")),
        name="modulation",
    )(cvec, w_mod, b_mod.reshape(L, 1, 3 * D_MODEL))
    return out.reshape(L, MOD_ROWS, 1, 3 * D_MODEL)


def _mod_row(batch):
    return lambda b, i: (jnp.where(i == 0, batch, b), 0, 0)


def _prenorm_kernel(x_ref, sh_ref, sc_ref, g_ref, h_ref):
    x = x_ref[...]
    y = x * lax.rsqrt(jnp.mean(x * x, axis=-1, keepdims=True) + NORM_EPS) * g_ref[...]
    h_ref[...] = (y * (1.0 + sc_ref[...]) + sh_ref[...]).astype(h_ref.dtype)


def _prenorm(xa, mod_l, g):
    B, N, _ = xa.shape
    row = _mod_row(B)
    return pl.pallas_call(
        _prenorm_kernel,
        grid=(B, N // TOK),
        in_specs=[
            pl.BlockSpec((None, TOK, D_MODEL), lambda b, i: (b, i, 0)),
            pl.BlockSpec((None, 1, D_MODEL), lambda b, i: row(b, i)[:2] + (0,)),
            pl.BlockSpec((None, 1, D_MODEL), lambda b, i: row(b, i)[:2] + (1,)),
            pl.BlockSpec((1, D_MODEL), lambda b, i: (0, 0)),
        ],
        out_specs=pl.BlockSpec((None, TOK, D_MODEL), lambda b, i: (b, i, 0)),
        out_shape=jax.ShapeDtypeStruct((B, N, D_MODEL), BF16),
        compiler_params=_cparams(("arbitrary", "arbitrary")),
        name="prenorm",
    )(xa, mod_l, mod_l, g.reshape(1, D_MODEL))


def _mm_kernel(a_ref, w_ref, o_ref):
    o_ref[...] = jnp.dot(a_ref[...], w_ref[...], preferred_element_type=F32).astype(o_ref.dtype)


def _project(h, w_all, li, tn, out_dtype, tm, shape3):
    M, K = h.shape
    ncols = w_all.shape[-1]
    out = pl.pallas_call(
        _mm_kernel,
        grid=(ncols // tn, M // tm),
        in_specs=[
            pl.BlockSpec((tm, K), lambda j, i: (i, 0)),
            pl.BlockSpec((None, K, tn), lambda j, i: (li, 0, j)),
        ],
        out_specs=pl.BlockSpec((tm, tn), lambda j, i: (i, j)),
        out_shape=jax.ShapeDtypeStruct((M, ncols), out_dtype),
        compiler_params=_cparams(("arbitrary", "arbitrary")),
        name="in_proj",
    )(h, w_all)
    return out.reshape(shape3)


def _conv_kernel(cur_ref, gate_ref, prev_ref, next_ref, cw_ref, cb_ref, lg_ref, lb_ref, o_ref, buf_ref, *, n_tiles):
    i = pl.program_id(1)

    def glu(t):
        return t[:, :BRANCH_W].astype(F32) * jax.nn.sigmoid(t[:, BRANCH_W:2 * BRANCH_W].astype(F32))

    has_prev = jnp.logical_and(i != 0, i != 1)
    has_next = jnp.logical_and(i != 0, i != n_tiles - 1)
    cur = cur_ref[...]
    buf_ref[0:CONV_HALO, :] = jnp.where(has_prev, glu(prev_ref[...]), 0.0)
    buf_ref[CONV_HALO:CONV_HALO + TOK, :] = glu(cur)
    buf_ref[CONV_HALO + TOK:, :] = jnp.where(has_next, glu(next_ref[...]), 0.0)
    base = CONV_HALO - CONV_K // 2
    acc = jnp.broadcast_to(cb_ref[...], (TOK, BRANCH_W))
    for r in range(SUBLANES):
        part = None
        for j in range(CONV_K):
            if (base + j) % SUBLANES == r:
                lo = base + j - r
                term = cw_ref[j:j + 1, :] * buf_ref[lo:lo + TOK + SUBLANES, :]
                part = term if part is None else part + term
        if part is not None:
            acc = acc + part[r:r + TOK]
    mu = jnp.mean(acc, axis=-1, keepdims=True)
    d = acc - mu
    var = jnp.mean(d * d, axis=-1, keepdims=True)
    y = d * lax.rsqrt(var + 1e-5) * lg_ref[...] + lb_ref[...]
    o_ref[...] = (_silu(y) * _silu(gate_ref[...].astype(F32))).astype(o_ref.dtype)


def _conv_branch(pa, conv_w, conv_b, ln_g, ln_b):
    B, N, _ = pa.shape
    nt = N // TOK
    hpt = TOK // CONV_HALO
    last = N // CONV_HALO - 1
    cblk = A_CONV // (2 * BRANCH_W)
    return pl.pallas_call(
        functools.partial(_conv_kernel, n_tiles=nt),
        grid=(B, nt),
        in_specs=[
            pl.BlockSpec((None, TOK, 2 * BRANCH_W), lambda b, i: (b, i, cblk)),
            pl.BlockSpec((None, TOK, BRANCH_W), lambda b, i: (b, i, 2 * cblk + 2)),
            pl.BlockSpec((None, CONV_HALO, 2 * BRANCH_W), lambda b, i: (b, jnp.maximum(i * hpt - 1, 0), cblk)),
            pl.BlockSpec((None, CONV_HALO, 2 * BRANCH_W), lambda b, i: (b, jnp.minimum((i + 1) * hpt, last), cblk)),
            pl.BlockSpec((CONV_K, BRANCH_W), lambda b, i: (0, 0)),
            pl.BlockSpec((1, BRANCH_W), lambda b, i: (0, 0)),
            pl.BlockSpec((1, BRANCH_W), lambda b, i: (0, 0)),
            pl.BlockSpec((1, BRANCH_W), lambda b, i: (0, 0)),
        ],
        out_specs=pl.BlockSpec((None, TOK, BRANCH_W), lambda b, i: (b, i, 0)),
        out_shape=jax.ShapeDtypeStruct((B, N, BRANCH_W), BF16),
        scratch_shapes=[pltpu.VMEM((TOK + 2 * CONV_HALO, BRANCH_W), F32)],
        compiler_params=_cparams(("arbitrary", "arbitrary")),
        name="conv_branch",
    )(pa, pa, pa, pa, conv_w, conv_b.reshape(1, -1), ln_g.reshape(1, -1), ln_b.reshape(1, -1))


def _rope_tables(n_ctx, n_lat, dim, reps):
    half = dim // 2
    t = np.arange(n_lat)
    inv = ROPE_THETA ** (-np.arange(0, half, 2, dtype=np.float64) / half)
    ang = np.concatenate([(t // GRID_W)[:, None] * inv, (t % GRID_W)[:, None] * inv], axis=-1)
    ang = np.concatenate([np.zeros((n_ctx, half)), ang], axis=0)
    cos = np.repeat(np.cos(ang), 2, axis=-1)
    sin = np.repeat(np.sin(ang), 2, axis=-1) * np.tile(np.array([-1.0, 1.0]), half)
    return (jnp.asarray(np.tile(cos, (1, reps)), F32), jnp.asarray(np.tile(sin, (1, reps)), F32))


def _rope(t, cos, sin_signed):
    lane = lax.broadcasted_iota(jnp.int32, t.shape, 1)
    partner = jnp.where(lane % 2 == 0, pltpu.roll(t, LANES - 1, 1), pltpu.roll(t, 1, 1))
    return t * cos + partner * sin_signed


LOG2_E = 1.4426950408889634


ATTN_CHUNK = 1536


def _scores_t(k, q):
    return lax.dot_general(k, q, (((1,), (1,)), ((), ())), preferred_element_type=F32)


def _online_attention(kp_ref, vt_ref, qs, nk):
    chunks = [(lo, min(ATTN_CHUNK, nk - lo)) for lo in range(0, nk, ATTN_CHUNK)]

    def scores(c):
        lo, n = chunks[c]
        k = kp_ref[lo:lo + n, :]
        return [_scores_t(k, q) for q in qs]

    state = [None] * len(qs)
    s_next = scores(0)
    for c, (lo, n) in enumerate(chunks):
        s_cur = s_next
        if c + 1 < len(chunks):
            s_next = scores(c + 1)
        v = vt_ref[:, lo:lo + n]
        for j, s in enumerate(s_cur):
            m = jnp.max(s, axis=0, keepdims=True)
            if state[j] is not None:
                m_old, l_old, acc_old = state[j]
                m = jnp.maximum(m_old, m)
            e = jnp.exp2(s - m)
            l = jnp.sum(e, axis=0, keepdims=True)
            acc = jnp.dot(v, e.astype(BF16), preferred_element_type=F32)
            if state[j] is not None:
                alpha = jnp.exp2(m_old - m)
                l = l_old * alpha + l
                acc = acc_old * alpha + acc
            state[j] = (m, l, acc)
    return [(acc, l) for _, l, acc in state]


def _gqa_kernel(q_ref, k_ref, v_ref, gate_ref, cos_ref, sin_ref, g_ref, o_ref, kp_ref, vt_ref, *, n_ctx, n_all):
    r = pl.program_id(2)
    qi = pl.program_id(3)

    @pl.when(jnp.logical_and(r == 0, qi == 0))
    def _():
        k = k_ref[...].astype(F32)
        k = k * lax.rsqrt(jnp.mean(k * k, axis=-1, keepdims=True) + NORM_EPS) * g_ref[1:2, :]
        kp_ref[...] = _rope(k, cos_ref[...], sin_ref[...]).astype(BF16)
        vt_ref[...] = v_ref[...].astype(F32).T.astype(BF16)

    rows = pl.ds(pl.multiple_of(qi * TOK, TOK), TOK)
    qs = []
    for j in range(GQA_PAIR):
        q = q_ref[:, j * GQA_HEAD:(j + 1) * GQA_HEAD].astype(F32)
        q = q * lax.rsqrt(jnp.mean(q * q, axis=-1, keepdims=True) + NORM_EPS) * g_ref[0:1, :]
        q = _rope(q, cos_ref[rows, :], sin_ref[rows, :]) * (GQA_HEAD ** -0.5 * LOG2_E)
        qs.append(q.astype(BF16))
    gate = _silu(gate_ref[...].astype(F32))

    def attend(nk):
        o = jnp.concatenate([(acc / l).T for acc, l in _online_attention(kp_ref, vt_ref, qs, nk)], axis=1)
        o_ref[...] = (o * gate).astype(o_ref.dtype)

    pl.when(qi == 0)(lambda: attend(n_ctx))
    pl.when(qi != 0)(lambda: attend(n_all))


GQA_PAIR = 2


def _gqa_branch(pb, qk_g, cos, sin, n_ctx):
    B, N, _ = pb.shape
    blk = lambda c0: c0 // GQA_HEAD
    wide = GQA_PAIR * GQA_HEAD
    steps = GQA_REP // GQA_PAIR
    return pl.pallas_call(
        functools.partial(_gqa_kernel, n_ctx=n_ctx, n_all=N),
        grid=(B, GQA_KV_HEADS, steps, N // TOK),
        in_specs=[
            pl.BlockSpec((None, TOK, wide), lambda b, g, r, i: (b, i, B_GQ_Q // wide + g * steps + r)),
            pl.BlockSpec((None, N, GQA_HEAD), lambda b, g, r, i: (b, 0, blk(B_GQ_K) + g)),
            pl.BlockSpec((None, N, GQA_HEAD), lambda b, g, r, i: (b, 0, blk(B_GQ_V) + g)),
            pl.BlockSpec((None, TOK, wide), lambda b, g, r, i: (b, i, B_GQ_G // wide + g * steps + r)),
            pl.BlockSpec((N, GQA_HEAD), lambda b, g, r, i: (0, 0)),
            pl.BlockSpec((N, GQA_HEAD), lambda b, g, r, i: (0, 0)),
            pl.BlockSpec((2, GQA_HEAD), lambda b, g, r, i: (0, 0)),
        ],
        out_specs=pl.BlockSpec((None, TOK, wide), lambda b, g, r, i: (b, i, g * steps + r)),
        out_shape=jax.ShapeDtypeStruct((B, N, BRANCH_W), BF16),
        scratch_shapes=[pltpu.VMEM((N, GQA_HEAD), BF16), pltpu.VMEM((GQA_HEAD, N), BF16)],
        compiler_params=_cparams(("arbitrary",) * 4),
        name="gqa_branch",
    )(pb, pb, pb, pb, cos, sin, qk_g)


def _diff_kernel(q_ref, k_ref, v_ref, gate_ref, cos_ref, sin_ref, lam_ref, sg_ref, o_ref, kp_ref, vt_ref,
                 *, n_ctx, n_all, lam_init):
    qi = pl.program_id(2)

    @pl.when(qi == 0)
    def _():
        kp_ref[...] = _rope(k_ref[...].astype(F32), cos_ref[...], sin_ref[...]).astype(BF16)
        vt_ref[...] = v_ref[...].astype(F32).T.astype(BF16)

    lp = lam_ref[...]
    lam = (jnp.exp(jnp.sum(lp[0:1] * lp[1:2], axis=-1, keepdims=True))
           - jnp.exp(jnp.sum(lp[2:3] * lp[3:4], axis=-1, keepdims=True)) + lam_init)
    rows = pl.ds(pl.multiple_of(qi * TOK, TOK), TOK)
    q = _rope(q_ref[...].astype(F32), cos_ref[rows, :], sin_ref[rows, :]) * (DIFF_HALF ** -0.5 * LOG2_E)
    lane = lax.broadcasted_iota(jnp.int32, q.shape, 1)
    q1 = jnp.where(lane < DIFF_HALF, q, 0.0).astype(BF16)
    q2 = jnp.where(lane >= DIFF_HALF, q, 0.0).astype(BF16)
    gate = _silu(gate_ref[...].astype(F32))

    def attend(nk):
        (acc1, l1), (acc2, l2) = _online_attention(kp_ref, vt_ref, (q1, q2), nk)
        o = (acc1 / l1 - acc2 * (lam / l2)).T
        o = o * lax.rsqrt(jnp.mean(o * o, axis=-1, keepdims=True) + NORM_EPS) * sg_ref[...]
        o_ref[...] = (o * (1.0 - lam_init) * gate).astype(o_ref.dtype)

    pl.when(qi == 0)(lambda: attend(n_ctx))
    pl.when(qi != 0)(lambda: attend(n_all))


def _diff_branch(pb, lam_p, subln_g, cos, sin, n_ctx, lam_init):
    B, N, _ = pb.shape
    blk = lambda c0: c0 // DIFF_VDIM
    return pl.pallas_call(
        functools.partial(_diff_kernel, n_ctx=n_ctx, n_all=N, lam_init=lam_init),
        grid=(B, DIFF_HEADS, N // TOK),
        in_specs=[
            pl.BlockSpec((None, TOK, DIFF_VDIM), lambda b, h, i: (b, i, blk(B_DF_Q) + h)),
            pl.BlockSpec((None, N, DIFF_VDIM), lambda b, h, i: (b, 0, blk(B_DF_K) + h)),
            pl.BlockSpec((None, N, DIFF_VDIM), lambda b, h, i: (b, 0, blk(B_DF_V) + h)),
            pl.BlockSpec((None, TOK, DIFF_VDIM), lambda b, h, i: (b, i, blk(B_DF_G) + h)),
            pl.BlockSpec((N, DIFF_VDIM), lambda b, h, i: (0, 0)),
            pl.BlockSpec((N, DIFF_VDIM), lambda b, h, i: (0, 0)),
            pl.BlockSpec((4, DIFF_HALF), lambda b, h, i: (0, 0)),
            pl.BlockSpec((1, DIFF_VDIM), lambda b, h, i: (0, 0)),
        ],
        out_specs=pl.BlockSpec((None, TOK, DIFF_VDIM), lambda b, h, i: (b, i, h)),
        out_shape=jax.ShapeDtypeStruct((B, N, BRANCH_W), BF16),
        scratch_shapes=[pltpu.VMEM((N, DIFF_VDIM), BF16), pltpu.VMEM((DIFF_VDIM, N), BF16)],
        compiler_params=_cparams(("arbitrary",) * 3),
        name="diff_branch",
    )(pb, pb, pb, pb, cos, sin, lam_p, subln_g.reshape(1, DIFF_VDIM))


XF_T = 128
SCAN_T = 64
SCAN_Q = RWKV_HEAD // 4
KEYED = ("w0", "kka0", "kd0", "w1", "kka1", "kd1", "r", "kk")
KEY_GROUPS = 4


def _rwkv_prep_kernel(r_ref, k_ref, v_ref, lora_ref, wup_hi_ref, wup_lo_ref, aup_hi_ref, aup_lo_ref, w0_ref, a0_ref,
                      kk_g_ref, ka_ref, rk_ref, ones_ref, *refs):
    out = dict(zip(KEYED, refs[:len(KEYED)]))
    zv_ref, bonus_ref, y_ref = refs[len(KEYED):]
    n_b = r_ref.shape[0]
    slot = {name: j * n_b for j, name in enumerate(KEYED + ("v",))}
    copies = LANES // (n_b * RWKV_HEADS)
    group = pl.program_id(1)

    @pl.when(group == 0)
    def _():
        _rwkv_prep_tile(r_ref, k_ref, v_ref, lora_ref, wup_hi_ref, wup_lo_ref, aup_hi_ref, aup_lo_ref, w0_ref, a0_ref,
                        kk_g_ref, ka_ref, rk_ref, ones_ref, bonus_ref, y_ref, slot)
        for val in range(SCAN_Q):
            rows = [y_ref[slot["v"] + b, pl.ds(q * SCAN_Q + val, RWKV_HEADS, stride=RWKV_HEAD), :]
                    for q in range(copies) for b in range(n_b)]
            zv_ref[:, val, :] = jnp.concatenate(rows, axis=0).T

    for g in range(KEY_GROUPS):
        @pl.when(group == g)
        def _(g=g):
            for name in KEYED:
                for j in range(RWKV_HEAD // KEY_GROUPS):
                    key = g * (RWKV_HEAD // KEY_GROUPS) + j
                    rows = [y_ref[slot[name] + b, pl.ds(key, RWKV_HEADS, stride=RWKV_HEAD), :] for b in range(n_b)]
                    out[name][j] = jnp.concatenate(rows * copies, axis=0).T


def _rwkv_prep_tile(r_ref, k_ref, v_ref, lora_ref, wup_hi_ref, wup_lo_ref, aup_hi_ref, aup_lo_ref, w0_ref, a0_ref,
                    kk_g_ref, ka_ref, rk_ref, ones_ref, bonus_ref, y_ref, slot):
    n_b = r_ref.shape[0]
    ones = ones_ref[...]
    for b in range(n_b):
        r = r_ref[b].astype(F32)
        k = k_ref[b].astype(F32)
        v = v_ref[b].astype(F32)
        lora = lora_ref[b]
        wl = jnp.tanh(lora)
        kk = k * kk_g_ref[...]
        kk = kk * lax.rsqrt(jnp.maximum(_split_dot(kk * kk, ones), 1e-24))
        y_ref[slot["r"] + b] = r.T
        y_ref[slot["kk"] + b] = kk.T
        y_ref[slot["v"] + b] = v.T
        bsum = jnp.zeros_like(r)
        for e in range(2):
            w_pre = w0_ref[e:e + 1, :] + _split_dot(wl, wup_hi_ref[e], wup_lo_ref[e])
            y_ref[slot[f"w{e}"] + b] = jnp.exp(-jax.nn.sigmoid(w_pre) * math.exp(-0.5)).T
            a = jax.nn.sigmoid(a0_ref[e:e + 1, :] + _split_dot(lora, aup_hi_ref[e], aup_lo_ref[e]))
            kd = k * (1.0 + (a - 1.0) * ka_ref[...])
            y_ref[slot[f"kd{e}"] + b] = kd.T
            y_ref[slot[f"kka{e}"] + b] = (kk * a).T
            bsum = bsum + kd
        bonus_ref[b] = _split_dot(r * bsum * rk_ref[...], ones) * v


def _rwkv_prep(pa, lora, wup, aup, w0, a0, k_k, k_a, r_k, ones):
    B, N, _ = pa.shape
    blk = lambda c0: c0 // BRANCH_W
    tok_spec = lambda c0: pl.BlockSpec((B, XF_T, BRANCH_W), lambda i, g: (0, i, blk(c0)))
    const = lambda shape: pl.BlockSpec(shape, lambda i, g: (0,) * len(shape), pipeline_mode=pl.Buffered(1))
    split = lambda t: (t.astype(BF16), (t - t.astype(BF16).astype(F32)).astype(BF16))
    wup_hi, wup_lo = split(wup)
    aup_hi, aup_lo = split(aup)
    keyed_shape = jax.ShapeDtypeStruct((N // XF_T, RWKV_HEAD, XF_T, LANES), F32)
    keyed_spec = pl.BlockSpec((None, RWKV_HEAD // KEY_GROUPS, XF_T, LANES), lambda i, g: (i, g, 0, 0))
    return pl.pallas_call(
        _rwkv_prep_kernel,
        grid=(N // XF_T, KEY_GROUPS),
        in_specs=[
            tok_spec(A_RK_R), tok_spec(A_RK_K), tok_spec(A_RK_V),
            pl.BlockSpec((B, XF_T, LORA_PAD), lambda i, g: (0, i, 0)),
            const((2, LORA_PAD, BRANCH_W)), const((2, LORA_PAD, BRANCH_W)),
            const((2, LORA_PAD, BRANCH_W)), const((2, LORA_PAD, BRANCH_W)),
            const((2, BRANCH_W)), const((2, BRANCH_W)),
            const((1, BRANCH_W)), const((1, BRANCH_W)), const((1, BRANCH_W)),
            const((BRANCH_W, BRANCH_W)),
        ],
        out_specs=[keyed_spec] * len(KEYED) + [
            pl.BlockSpec((XF_T, SCAN_Q, LANES), lambda i, g: (i, 0, 0)),
            pl.BlockSpec((B, XF_T, BRANCH_W), lambda i, g: (0, i, 0))],
        out_shape=[keyed_shape] * len(KEYED) + [
            jax.ShapeDtypeStruct((N, SCAN_Q, LANES), F32),
            jax.ShapeDtypeStruct((B, N, BRANCH_W), F32)],
        scratch_shapes=[pltpu.VMEM(((len(KEYED) + 1) * B, BRANCH_W, XF_T), F32)],
        compiler_params=_cparams(("arbitrary", "arbitrary")),
        name="rwkv_prep",
    )(pa, pa, pa, lora, wup_hi, wup_lo, aup_hi, aup_lo, w0, a0,
      k_k.reshape(1, -1), k_a.reshape(1, -1), r_k.reshape(1, -1), ones)


def _rwkv_scan_kernel(wf, kkaf, kdf, wb, kkab, kdb, rf, kkf, rb, kkb, kkn_f, kkn_b, vf, vb,
                      of_ref, ob_ref, ckf, ckb, sf_ref, sb_ref, saf_ref, sab_ref):
    @pl.when(pl.program_id(0) == 0)
    def _():
        for ref in (sf_ref, sb_ref, saf_ref, sab_ref):
            ref[...] = jnp.zeros_like(ref)

    ckf[:, 0:SCAN_T, :] = kkf[...]
    ckf[:, SCAN_T:, :] = kkn_f[...]
    ckb[:, 0:SUBLANES, :] = kkn_b[...]
    ckb[:, SUBLANES:, :] = kkb[...]

    def chain(s_ref, sa, v_t, w, kka, kd, r, kk_next, t, t_next):
        o = jnp.zeros((SCAN_Q, LANES), F32)
        nsa = jnp.zeros((SCAN_Q, LANES), F32)
        row = pl.ds(t, 1)
        row_next = pl.ds(t_next, 1)
        for k in range(RWKV_HEAD):
            s = s_ref[k] * w[k, row, :] + sa * kka[k, row, :] + v_t * kd[k, row, :]
            s_ref[k] = s
            o = o + s * r[k, row, :]
            nsa = nsa + s * kk_next[k, row_next, :]
        return o, -nsa

    def step(t, carry):
        sa_f, sa_b = carry
        tb = SCAN_T - 1 - t
        o_f, sa_f = chain(sf_ref, sa_f, vf[t], wf, kkaf, kdf, rf, ckf, t, t + 1)
        o_b, sa_b = chain(sb_ref, sa_b, vb[tb], wb, kkab, kdb, rb, ckb, tb, tb + SUBLANES - 1)
        of_ref[t] = o_f
        ob_ref[tb] = o_b
        return sa_f, sa_b

    saf_ref[...], sab_ref[...] = lax.fori_loop(0, SCAN_T, step, (saf_ref[...], sab_ref[...]))


def _rwkv_scan(z, zv, n_ctx):
    N = zv.shape[0]
    nb, nc = N // SCAN_T, n_ctx // SCAN_T
    per_tile = XF_T // SCAN_T

    def mirror(i):
        return jnp.where(i < nc, nc - 1 - i, nb + nc - 1 - i)

    def tok_f(i):
        return jnp.minimum((i + 1) * SCAN_T, N - SUBLANES)

    def tok_b(i):
        return mirror(jnp.minimum(i + 1, nb - 1)) * SCAN_T + SCAN_T - 1

    def keyed(block_of):
        return pl.BlockSpec((None, RWKV_HEAD, SCAN_T, LANES),
                            lambda i: (block_of(i) // per_tile, 0, block_of(i) % per_tile, 0))

    def group(tok_of):
        return pl.BlockSpec((None, RWKV_HEAD, SUBLANES, LANES),
                            lambda i: (tok_of(i) // XF_T, 0, (tok_of(i) % XF_T) // SUBLANES, 0))

    fwd, bwd = keyed(lambda i: i), keyed(mirror)
    vf = pl.BlockSpec((SCAN_T, SCAN_Q, LANES), lambda i: (i, 0, 0))
    vb = pl.BlockSpec((SCAN_T, SCAN_Q, LANES), lambda i: (mirror(i), 0, 0))
    out = jax.ShapeDtypeStruct((N, SCAN_Q, LANES), F32)
    state = pltpu.VMEM((RWKV_HEAD, SCAN_Q, LANES), F32)
    return pl.pallas_call(
        _rwkv_scan_kernel,
        grid=(nb,),
        in_specs=[fwd] * 3 + [bwd] * 3 + [fwd, fwd, bwd, bwd, group(tok_f), group(tok_b), vf, vb],
        out_specs=[vf, vb],
        out_shape=[out, out],
        scratch_shapes=[pltpu.VMEM((RWKV_HEAD, SCAN_T + SUBLANES, LANES), F32)] * 2 + [state, state] + [
            pltpu.VMEM((SCAN_Q, LANES), F32)] * 2,
        compiler_params=_cparams(("arbitrary",)),
        name="rwkv_scan",
    )(z["w0"], z["kka0"], z["kd0"], z["w1"], z["kka1"], z["kd1"], z["r"], z["kk"], z["r"], z["kk"],
      z["kk"], z["kk"], zv, zv)


def _rwkv_post_kernel(of_ref, ob_ref, bonus_ref, gate_ref, g_ref, b_ref, ones_ref, o_ref, y_ref):
    n_b = bonus_ref.shape[0]
    copies = LANES // (n_b * RWKV_HEADS)
    for val in range(SCAN_Q):
        t = (of_ref[:, val, :] + ob_ref[:, val, :]).T
        for q in range(copies):
            for b in range(n_b):
                lo = (q * n_b + b) * RWKV_HEADS
                y_ref[b, pl.ds(q * SCAN_Q + val, RWKV_HEADS, stride=RWKV_HEAD), :] = t[lo:lo + RWKV_HEADS]
    ones = ones_ref[...]
    for b in range(n_b):
        y = y_ref[b].T
        mu = _split_dot(y, ones) * (1.0 / RWKV_HEAD)
        d = y - mu
        var = _split_dot(d * d, ones) * (1.0 / RWKV_HEAD)
        yn = d * lax.rsqrt(var + RWKV_GN_EPS) * g_ref[...] + b_ref[...]
        o_ref[b] = ((yn + bonus_ref[b]) * _silu(gate_ref[b].astype(F32))).astype(o_ref.dtype)


def _rwkv_post(o_f, o_b, bonus, pb, gn_g, gn_b, ones):
    B, N, _ = bonus.shape
    nat = pl.BlockSpec((B, XF_T, BRANCH_W), lambda i: (0, i, 0))
    scan = pl.BlockSpec((XF_T, SCAN_Q, LANES), lambda i: (i, 0, 0))
    const = lambda shape: pl.BlockSpec(shape, lambda i: (0, 0))
    return pl.pallas_call(
        _rwkv_post_kernel,
        grid=(N // XF_T,),
        in_specs=[scan, scan, nat,
                  pl.BlockSpec((B, XF_T, BRANCH_W), lambda i: (0, i, B_RK_G // BRANCH_W)),
                  const((1, BRANCH_W)), const((1, BRANCH_W)), const((BRANCH_W, BRANCH_W))],
        out_specs=nat,
        out_shape=jax.ShapeDtypeStruct((B, N, BRANCH_W), BF16),
        scratch_shapes=[pltpu.VMEM((B, BRANCH_W, XF_T), F32)],
        compiler_params=_cparams(("arbitrary",)),
        name="rwkv_post",
    )(o_f, o_b, bonus, pb, gn_g.reshape(1, -1), gn_b.reshape(1, -1), ones)


def _rwkv_branch(pa, pb, lora, li, p, ones, n_ctx):
    B, N, _ = pa.shape
    assert LANES % (B * RWKV_HEADS) == 0 and LANES // (B * RWKV_HEADS) * SCAN_Q == RWKV_HEAD
    assert n_ctx % SCAN_T == 0 and N % XF_T == 0 and XF_T % SCAN_T == 0
    outs = _rwkv_prep(pa, lora, p["wup"][li], p["aup"][li], p["rwkv_w0"][li], p["rwkv_a0"][li],
                      p["rwkv_k_k"][li], p["rwkv_k_a"][li], p["rwkv_r_k"][li], ones)
    z = dict(zip(KEYED, outs[:len(KEYED)]))
    zv, bonus = outs[len(KEYED):]
    o_f, o_b = _rwkv_scan(z, zv, n_ctx)
    return _rwkv_post(o_f, o_b, bonus, pb, p["rwkv_gn_g"][li], p["rwkv_gn_b"][li], ones)


def _merge_kernel(bc_ref, br_ref, bd_ref, bq_ref, lg_ref, wb_ref, bg_ref, wo_ref, x_ref, gp_ref, gt_ref,
                  sh_ref, sc_ref, gn_ref, o_ref, h_ref):
    merged = jnp.zeros((TOK, D_MODEL), F32)
    for j, b_ref in enumerate((bc_ref, br_ref, bd_ref, bq_ref)):
        pj = jnp.dot(b_ref[...], wb_ref[j], preferred_element_type=F32)
        gj = jax.nn.sigmoid(lg_ref[:, j * D_MODEL:(j + 1) * D_MODEL].astype(F32) + bg_ref[j:j + 1, :])
        merged = merged + pj * gj
    y = jnp.dot(merged.astype(BF16), wo_ref[...], preferred_element_type=F32)
    y = y * lax.rsqrt(jnp.mean(y * y, axis=-1, keepdims=True) + NORM_EPS) * gp_ref[...]
    x = x_ref[...] + gt_ref[...] * y
    o_ref[...] = x
    hn = x * lax.rsqrt(jnp.mean(x * x, axis=-1, keepdims=True) + NORM_EPS) * gn_ref[...]
    h_ref[...] = (hn * (1.0 + sc_ref[...]) + sh_ref[...]).astype(h_ref.dtype)


def _merge(branches, pc, w_branch, b_gate, w_out, li, xa, g_post, mod_l, mod_next, g_next):
    B, N, _ = xa.shape
    row = _mod_row(B)
    bspec = pl.BlockSpec((None, TOK, BRANCH_W), lambda b, i: (b, i, 0))
    xspec = pl.BlockSpec((None, TOK, D_MODEL), lambda b, i: (b, i, 0))
    mod_spec = lambda part: pl.BlockSpec((None, 1, D_MODEL), lambda b, i: row(b, i)[:2] + (part,))
    vec_spec = pl.BlockSpec((1, D_MODEL), lambda b, i: (0, 0))
    once = pl.Buffered(1)
    return pl.pallas_call(
        _merge_kernel,
        grid=(B, N // TOK),
        in_specs=[bspec] * 4 + [
            pl.BlockSpec((None, TOK, N_BRANCH * D_MODEL), lambda b, i: (b, i, 0)),
            pl.BlockSpec((None, N_BRANCH, BRANCH_W, D_MODEL), lambda b, i: (li, 0, 0, 0), pipeline_mode=once),
            pl.BlockSpec((None, N_BRANCH, D_MODEL), lambda b, i: (li, 0, 0)),
            pl.BlockSpec((None, D_MODEL, D_MODEL), lambda b, i: (li, 0, 0), pipeline_mode=once),
            xspec, vec_spec, mod_spec(2), mod_spec(0), mod_spec(1), vec_spec,
        ],
        out_specs=[xspec, xspec],
        out_shape=[jax.ShapeDtypeStruct((B, N, D_MODEL), F32), jax.ShapeDtypeStruct((B, N, D_MODEL), BF16)],
        compiler_params=_cparams(("arbitrary", "arbitrary")),
        name="merge",
    )(*branches, pc, w_branch, b_gate, w_out, xa, g_post.reshape(1, -1), mod_l, mod_next, mod_next,
      g_next.reshape(1, -1))


def kernel(x, c, ctx, c_ctx, norm_pre_g, norm_post_g, w_mod, b_mod, w_in, conv_w, conv_b, conv_ln_g, conv_ln_b,
           rwkv_w0, rwkv_w_up, rwkv_a0, rwkv_a_up, rwkv_k_k, rwkv_k_a, rwkv_r_k, rwkv_gn_g, rwkv_gn_b,
           diff_lam, diff_subln_g, gqa_qk_norm_g, w_branch, b_gate, w_out):
    B, n_lat, _ = x.shape
    n_ctx = ctx.shape[1]
    depth = w_in.shape[0]
    N = n_ctx + n_lat
    assert n_ctx == TOK and n_lat % TOK == 0 and B + 1 <= MOD_ROWS and w_in.shape[-1] == D_IN

    w_a = w_in[..., O_A:O_LORA].astype(BF16)
    w_b = w_in[..., O_B:O_C].astype(BF16)
    w_c = w_in[..., O_C:].astype(BF16)
    w_l = jnp.pad(w_in[..., O_LORA:O_B], ((0, 0), (0, 0), (0, LORA_PAD - DECAY_LORA - AAA_LORA))).astype(BF16)
    w_branch_b = w_branch.astype(BF16)
    w_out_b = w_out.astype(BF16)
    zrows = lambda t, lo: jnp.pad(t, ((0, 0), (0, 0), (lo, LORA_PAD - lo - t.shape[2]), (0, 0)))
    params = dict(wup=zrows(rwkv_w_up, 0), aup=zrows(rwkv_a_up, DECAY_LORA), rwkv_w0=rwkv_w0, rwkv_a0=rwkv_a0,
                  rwkv_k_k=rwkv_k_k, rwkv_k_a=rwkv_k_a, rwkv_r_k=rwkv_r_k.reshape(depth, BRANCH_W),
                  rwkv_gn_g=rwkv_gn_g, rwkv_gn_b=rwkv_gn_b)
    head = np.arange(BRANCH_W) // RWKV_HEAD
    ones = jnp.asarray(head[:, None] == head[None, :], BF16)
    cos_d, sin_d = _rope_tables(n_ctx, n_lat, DIFF_HALF, 2)
    cos_g, sin_g = _rope_tables(n_ctx, n_lat, GQA_HEAD, 1)

    cvec = jnp.concatenate([c, c_ctx[None, :], jnp.zeros((MOD_ROWS - B - 1, D_MODEL), c.dtype)], axis=0)
    mod = _modulation(cvec, w_mod, b_mod)
    xa = jnp.concatenate([ctx, x], axis=1)
    tm = 2176 if (B * N) % 2176 == 0 else TOK

    h = _prenorm(xa, mod[0], norm_pre_g[0])
    for li in range(depth):
        h = h.reshape(B * N, D_MODEL)
        pa = _project(h, w_a, li, 1024, BF16, tm, (B, N, O_LORA - O_A))
        pb = _project(h, w_b, li, 768, BF16, tm, (B, N, O_C - O_B))
        pc = _project(h, w_c, li, 1024, BF16, tm, (B, N, D_IN - O_C))
        lora = _project(h, w_l, li, LORA_PAD, F32, tm, (B, N, LORA_PAD))
        lam_init = 0.8 - 0.6 * math.exp(-0.3 * li)
        br_conv = _conv_branch(pa, conv_w[li], conv_b[li], conv_ln_g[li], conv_ln_b[li])
        br_rwkv = _rwkv_branch(pa, pb, lora, li, params, ones, n_ctx)
        br_diff = _diff_branch(pb, diff_lam[li], diff_subln_g[li], cos_d, sin_d, n_ctx, lam_init)
        br_gqa = _gqa_branch(pb, gqa_qk_norm_g[li], cos_g, sin_g, n_ctx)
        nxt = min(li + 1, depth - 1)
        xa, h = _merge((br_conv, br_rwkv, br_diff, br_gqa), pc, w_branch_b, b_gate, w_out_b, li, xa,
                       norm_post_g[li], mod[li], mod[nxt], norm_pre_g[nxt])
    return xa[:, n_ctx:]
```

```python
import functools
import math

import jax
import jax.numpy as jnp
import numpy as np
from jax import lax
from jax.experimental import pallas as pl
from jax.experimental.pallas import tpu as pltpu

F32 = jnp.float32
BF16 = jnp.bfloat16

D_MODEL = 2048
BRANCH_W = D_MODEL // 2
N_BRANCH = 4
GRID_W = 64
CONV_K = 31
CONV_HALO = 16
RWKV_HEAD = 64
RWKV_HEADS = BRANCH_W // RWKV_HEAD
DECAY_LORA = 96
AAA_LORA = 96
LORA_PAD = 256
RWKV_GN_EPS = 64e-5
DIFF_HALF = 64
DIFF_VDIM = 2 * DIFF_HALF
DIFF_HEADS = BRANCH_W // DIFF_VDIM
GQA_HEAD = 128
GQA_HEADS = BRANCH_W // GQA_HEAD
GQA_KV_HEADS = GQA_HEADS // 4
GQA_REP = GQA_HEADS // GQA_KV_HEADS
ROPE_THETA = 10000.0
NORM_EPS = 1e-6

TOK = 256
LANES = 128
SUBLANES = 8
MOD_ROWS = 8
VMEM_LIMIT = 60 * 1024 * 1024

O_A = 0
O_LORA = 6 * BRANCH_W
O_B = O_LORA + DECAY_LORA + AAA_LORA
O_C = O_B + 6 * BRANCH_W + 2 * GQA_KV_HEADS * GQA_HEAD + BRANCH_W
D_IN = O_C + N_BRANCH * D_MODEL
A_CONV, A_RK_R, A_RK_K, A_RK_V = 0, 3 * BRANCH_W, 4 * BRANCH_W, 5 * BRANCH_W
B_RK_G, B_DF_Q, B_DF_K, B_DF_V, B_DF_G, B_GQ_Q = (j * BRANCH_W for j in range(6))
B_GQ_K = 6 * BRANCH_W
B_GQ_V = B_GQ_K + GQA_KV_HEADS * GQA_HEAD
B_GQ_G = B_GQ_V + GQA_KV_HEADS * GQA_HEAD


def _cparams(sem):
    return pltpu.CompilerParams(dimension_semantics=sem, vmem_limit_bytes=VMEM_LIMIT)


def _silu(t):
    return t * jax.nn.sigmoid(t)


def _split_dot(a, b_hi, b_lo=None):
    a_hi = a.astype(BF16)
    a_lo = (a - a_hi.astype(F32)).astype(BF16)
    out = jnp.dot(a_hi, b_hi, preferred_element_type=F32) + jnp.dot(a_lo, b_hi, preferred_element_type=F32)
    if b_lo is not None:
        out = out + jnp.dot(a_hi, b_lo, preferred_element_type=F32)
    return out


def _mod_kernel(c_ref, w_ref, b_ref, o_ref):
    cv = c_ref[...]
    s = _silu(cv).astype(BF16)
    o_ref[...] = jnp.dot(s, w_ref[...].astype(BF16), preferred_element_type=F32) + b_ref[...]


def _modulation(cvec, w_mod, b_mod):
    L = w_mod.shape[0]
    tn = 1536
    out = pl.pallas_call(
        _mod_kernel,
        grid=(L, 3 * D_MODEL // tn),
        in_specs=[
            pl.BlockSpec((MOD_ROWS, D_MODEL), lambda l, j: (0, 0)),
            pl.BlockSpec((None, D_MODEL, tn), lambda l, j: (l, 0, j)),
            pl.BlockSpec((None, 1, tn), lambda l, j: (l, 0, j)),
        ],
        out_specs=pl.BlockSpec((None, MOD_ROWS, tn), lambda l, j: (l, 0, j)),
        out_shape=jax.ShapeDtypeStruct((L, MOD_ROWS, 3 * D_MODEL), F32),
        compiler_params=_cparams(("arbitrary", "arbitrary")),
        name="modulation",
    )(cvec, w_mod, b_mod.reshape(L, 1, 3 * D_MODEL))
    return out.reshape(L, MOD_ROWS, 1, 3 * D_MODEL)


def _mod_row(batch):
    return lambda b, i: (jnp.where(i == 0, batch, b), 0, 0)


def _prenorm_kernel(x_ref, sh_ref, sc_ref, g_ref, h_ref):
    x = x_ref[...]
    y = x * lax.rsqrt(jnp.mean(x * x, axis=-1, keepdims=True) + NORM_EPS) * g_ref[...]
    h_ref[...] = (y * (1.0 + sc_ref[...]) + sh_ref[...]).astype(h_ref.dtype)


def _prenorm(xa, mod_l, g):
    B, N, _ = xa.shape
    row = _mod_row(B)
    return pl.pallas_call(
        _prenorm_kernel,
        grid=(B, N // TOK),
        in_specs=[
            pl.BlockSpec((None, TOK, D_MODEL), lambda b, i: (b, i, 0)),
            pl.BlockSpec((None, 1, D_MODEL), lambda b, i: row(b, i)[:2] + (0,)),
            pl.BlockSpec((None, 1, D_MODEL), lambda b, i: row(b, i)[:2] + (1,)),
            pl.BlockSpec((1, D_MODEL), lambda b, i: (0, 0)),
        ],
        out_specs=pl.BlockSpec((None, TOK, D_MODEL), lambda b, i: (b, i, 0)),
        out_shape=jax.ShapeDtypeStruct((B, N, D_MODEL), BF16),
        compiler_params=_cparams(("arbitrary", "arbitrary")),
        name="prenorm",
    )(xa, mod_l, mod_l, g.reshape(1, D_MODEL))


def _mm_kernel(a_ref, w_ref, o_ref):
    o_ref[...] = jnp.dot(a_ref[...], w_ref[...], preferred_element_type=F32).astype(o_ref.dtype)


def _mm_shift_kernel(a_ref, w_lo_ref, w_hi_ref, o_ref, w_ref, *, shift):
    @pl.when(pl.program_id(1) == 0)
    def _():
        w_ref[...] = jnp.concatenate([w_lo_ref[:, shift:], w_hi_ref[:, :shift]], axis=1)

    o_ref[...] = jnp.dot(a_ref[...], w_ref[...], preferred_element_type=F32).astype(o_ref.dtype)


def _project(h, w_all, li, col0, ncols, tn, out_dtype, tm, shape3):
    M, K = h.shape
    first, shift = divmod(col0, tn)
    w_spec = lambda off: pl.BlockSpec((None, K, tn), lambda j, i: (li, 0, first + off + j))
    if shift == 0:
        body, w_specs, scratch = _mm_kernel, [w_spec(0)], []
    else:
        body = functools.partial(_mm_shift_kernel, shift=shift)
        w_specs, scratch = [w_spec(0), w_spec(1)], [pltpu.VMEM((K, tn), w_all.dtype)]
    out = pl.pallas_call(
        body,
        grid=(ncols // tn, M // tm),
        in_specs=[pl.BlockSpec((tm, K), lambda j, i: (i, 0))] + w_specs,
        out_specs=pl.BlockSpec((tm, tn), lambda j, i: (i, j)),
        out_shape=jax.ShapeDtypeStruct((M, ncols), out_dtype),
        scratch_shapes=scratch,
        compiler_params=_cparams(("arbitrary", "arbitrary")),
        name="in_proj",
    )(h, *([w_all] * len(w_specs)))
    return out.reshape(shape3)


def _conv_kernel(cur_ref, gate_ref, prev_ref, next_ref, cw_ref, cb_ref, lg_ref, lb_ref, o_ref, buf_ref, *, n_tiles):
    i = pl.program_id(1)

    def glu(t):
        return t[:, :BRANCH_W].astype(F32) * jax.nn.sigmoid(t[:, BRANCH_W:2 * BRANCH_W].astype(F32))

    has_prev = jnp.logical_and(i != 0, i != 1)
    has_next = jnp.logical_and(i != 0, i != n_tiles - 1)
    cur = cur_ref[...]
    buf_ref[0:CONV_HALO, :] = jnp.where(has_prev, glu(prev_ref[...]), 0.0)
    buf_ref[CONV_HALO:CONV_HALO + TOK, :] = glu(cur)
    buf_ref[CONV_HALO + TOK:, :] = jnp.where(has_next, glu(next_ref[...]), 0.0)
    base = CONV_HALO - CONV_K // 2
    acc = jnp.broadcast_to(cb_ref[...], (TOK, BRANCH_W))
    for r in range(SUBLANES):
        part = None
        for j in range(CONV_K):
            if (base + j) % SUBLANES == r:
                lo = base + j - r
                term = cw_ref[j:j + 1, :] * buf_ref[lo:lo + TOK + SUBLANES, :]
                part = term if part is None else part + term
        if part is not None:
            acc = acc + part[r:r + TOK]
    mu = jnp.mean(acc, axis=-1, keepdims=True)
    d = acc - mu
    var = jnp.mean(d * d, axis=-1, keepdims=True)
    y = d * lax.rsqrt(var + 1e-5) * lg_ref[...] + lb_ref[...]
    o_ref[...] = (_silu(y) * _silu(gate_ref[...].astype(F32))).astype(o_ref.dtype)


def _conv_branch(pa, conv_w, conv_b, ln_g, ln_b):
    B, N, _ = pa.shape
    nt = N // TOK
    hpt = TOK // CONV_HALO
    last = N // CONV_HALO - 1
    cblk = A_CONV // (2 * BRANCH_W)
    return pl.pallas_call(
        functools.partial(_conv_kernel, n_tiles=nt),
        grid=(B, nt),
        in_specs=[
            pl.BlockSpec((None, TOK, 2 * BRANCH_W), lambda b, i: (b, i, cblk)),
            pl.BlockSpec((None, TOK, BRANCH_W), lambda b, i: (b, i, 2 * cblk + 2)),
            pl.BlockSpec((None, CONV_HALO, 2 * BRANCH_W), lambda b, i: (b, jnp.maximum(i * hpt - 1, 0), cblk)),
            pl.BlockSpec((None, CONV_HALO, 2 * BRANCH_W), lambda b, i: (b, jnp.minimum((i + 1) * hpt, last), cblk)),
            pl.BlockSpec((CONV_K, BRANCH_W), lambda b, i: (0, 0)),
            pl.BlockSpec((1, BRANCH_W), lambda b, i: (0, 0)),
            pl.BlockSpec((1, BRANCH_W), lambda b, i: (0, 0)),
            pl.BlockSpec((1, BRANCH_W), lambda b, i: (0, 0)),
        ],
        out_specs=pl.BlockSpec((None, TOK, BRANCH_W), lambda b, i: (b, i, 0)),
        out_shape=jax.ShapeDtypeStruct((B, N, BRANCH_W), BF16),
        scratch_shapes=[pltpu.VMEM((TOK + 2 * CONV_HALO, BRANCH_W), F32)],
        compiler_params=_cparams(("arbitrary", "arbitrary")),
        name="conv_branch",
    )(pa, pa, pa, pa, conv_w, conv_b.reshape(1, -1), ln_g.reshape(1, -1), ln_b.reshape(1, -1))


def _rope_tables(n_ctx, n_lat, dim, reps):
    half = dim // 2
    t = np.arange(n_lat)
    inv = ROPE_THETA ** (-np.arange(0, half, 2, dtype=np.float64) / half)
    ang = np.concatenate([(t // GRID_W)[:, None] * inv, (t % GRID_W)[:, None] * inv], axis=-1)
    ang = np.concatenate([np.zeros((n_ctx, half)), ang], axis=0)
    cos = np.repeat(np.cos(ang), 2, axis=-1)
    sin = np.repeat(np.sin(ang), 2, axis=-1) * np.tile(np.array([-1.0, 1.0]), half)
    return (jnp.asarray(np.tile(cos, (1, reps)), F32), jnp.asarray(np.tile(sin, (1, reps)), F32))


def _rope(t, cos, sin_signed):
    lane = lax.broadcasted_iota(jnp.int32, t.shape, 1)
    partner = jnp.where(lane % 2 == 0, pltpu.roll(t, LANES - 1, 1), pltpu.roll(t, 1, 1))
    return t * cos + partner * sin_signed


LOG2_E = 1.4426950408889634


ATTN_CHUNK = 1536


def _scores_t(k, q):
    return lax.dot_general(k, q, (((1,), (1,)), ((), ())), preferred_element_type=F32)


def _online_attention(kp_ref, vt_ref, qs, nk):
    chunks = [(lo, min(ATTN_CHUNK, nk - lo)) for lo in range(0, nk, ATTN_CHUNK)]

    def scores(c):
        lo, n = chunks[c]
        k = kp_ref[lo:lo + n, :]
        return [_scores_t(k, q) for q in qs]

    state = [None] * len(qs)
    s_next = scores(0)
    for c, (lo, n) in enumerate(chunks):
        s_cur = s_next
        if c + 1 < len(chunks):
            s_next = scores(c + 1)
        v = vt_ref[:, lo:lo + n]
        for j, s in enumerate(s_cur):
            m = jnp.max(s, axis=0, keepdims=True)
            if state[j] is not None:
                m_old, l_old, acc_old = state[j]
                m = jnp.maximum(m_old, m)
            e = jnp.exp2(s - m)
            l = jnp.sum(e, axis=0, keepdims=True)
            acc = jnp.dot(v, e.astype(BF16), preferred_element_type=F32)
            if state[j] is not None:
                alpha = jnp.exp2(m_old - m)
                l = l_old * alpha + l
                acc = acc_old * alpha + acc
            state[j] = (m, l, acc)
    return [(acc, l) for _, l, acc in state]


def _gqa_kernel(q_ref, k_ref, v_ref, gate_ref, cos_ref, sin_ref, g_ref, o_ref, kp_ref, vt_ref, *, n_ctx, n_all):
    r = pl.program_id(2)
    qi = pl.program_id(3)

    @pl.when(jnp.logical_and(r == 0, qi == 0))
    def _():
        k = k_ref[...].astype(F32)
        k = k * lax.rsqrt(jnp.mean(k * k, axis=-1, keepdims=True) + NORM_EPS) * g_ref[1:2, :]
        kp_ref[...] = _rope(k, cos_ref[...], sin_ref[...]).astype(BF16)
        vt_ref[...] = v_ref[...].astype(F32).T.astype(BF16)

    rows = pl.ds(pl.multiple_of(qi * TOK, TOK), TOK)
    qs = []
    for j in range(GQA_PAIR):
        q = q_ref[:, j * GQA_HEAD:(j + 1) * GQA_HEAD].astype(F32)
        q = q * lax.rsqrt(jnp.mean(q * q, axis=-1, keepdims=True) + NORM_EPS) * g_ref[0:1, :]
        q = _rope(q, cos_ref[rows, :], sin_ref[rows, :]) * (GQA_HEAD ** -0.5 * LOG2_E)
        qs.append(q.astype(BF16))
    gate = _silu(gate_ref[...].astype(F32))

    def attend(nk):
        o = jnp.concatenate([(acc / l).T for acc, l in _online_attention(kp_ref, vt_ref, qs, nk)], axis=1)
        o_ref[...] = (o * gate).astype(o_ref.dtype)

    pl.when(qi == 0)(lambda: attend(n_ctx))
    pl.when(qi != 0)(lambda: attend(n_all))


GQA_PAIR = 2


def _gqa_branch(pb, qk_g, cos, sin, n_ctx):
    B, N, _ = pb.shape
    blk = lambda c0: c0 // GQA_HEAD
    wide = GQA_PAIR * GQA_HEAD
    steps = GQA_REP // GQA_PAIR
    return pl.pallas_call(
        functools.partial(_gqa_kernel, n_ctx=n_ctx, n_all=N),
        grid=(B, GQA_KV_HEADS, steps, N // TOK),
        in_specs=[
            pl.BlockSpec((None, TOK, wide), lambda b, g, r, i: (b, i, B_GQ_Q // wide + g * steps + r)),
            pl.BlockSpec((None, N, GQA_HEAD), lambda b, g, r, i: (b, 0, blk(B_GQ_K) + g)),
            pl.BlockSpec((None, N, GQA_HEAD), lambda b, g, r, i: (b, 0, blk(B_GQ_V) + g)),
            pl.BlockSpec((None, TOK, wide), lambda b, g, r, i: (b, i, B_GQ_G // wide + g * steps + r)),
            pl.BlockSpec((N, GQA_HEAD), lambda b, g, r, i: (0, 0)),
            pl.BlockSpec((N, GQA_HEAD), lambda b, g, r, i: (0, 0)),
            pl.BlockSpec((2, GQA_HEAD), lambda b, g, r, i: (0, 0)),
        ],
        out_specs=pl.BlockSpec((None, TOK, wide), lambda b, g, r, i: (b, i, g * steps + r)),
        out_shape=jax.ShapeDtypeStruct((B, N, BRANCH_W), BF16),
        scratch_shapes=[pltpu.VMEM((N, GQA_HEAD), BF16), pltpu.VMEM((GQA_HEAD, N), BF16)],
        compiler_params=_cparams(("arbitrary",) * 4),
        name="gqa_branch",
    )(pb, pb, pb, pb, cos, sin, qk_g)


def _diff_kernel(q_ref, k_ref, v_ref, gate_ref, cos_ref, sin_ref, lam_ref, sg_ref, o_ref, kp_ref, vt_ref,
                 *, n_ctx, n_all, lam_init):
    qi = pl.program_id(2)

    @pl.when(qi == 0)
    def _():
        kp_ref[...] = _rope(k_ref[...].astype(F32), cos_ref[...], sin_ref[...]).astype(BF16)
        vt_ref[...] = v_ref[...].astype(F32).T.astype(BF16)

    lp = lam_ref[...]
    lam = (jnp.exp(jnp.sum(lp[0:1] * lp[1:2], axis=-1, keepdims=True))
           - jnp.exp(jnp.sum(lp[2:3] * lp[3:4], axis=-1, keepdims=True)) + lam_init)
    rows = pl.ds(pl.multiple_of(qi * TOK, TOK), TOK)
    q = _rope(q_ref[...].astype(F32), cos_ref[rows, :], sin_ref[rows, :]) * (DIFF_HALF ** -0.5 * LOG2_E)
    lane = lax.broadcasted_iota(jnp.int32, q.shape, 1)
    q1 = jnp.where(lane < DIFF_HALF, q, 0.0).astype(BF16)
    q2 = jnp.where(lane >= DIFF_HALF, q, 0.0).astype(BF16)
    gate = _silu(gate_ref[...].astype(F32))

    def attend(nk):
        (acc1, l1), (acc2, l2) = _online_attention(kp_ref, vt_ref, (q1, q2), nk)
        o = (acc1 / l1 - acc2 * (lam / l2)).T
        o = o * lax.rsqrt(jnp.mean(o * o, axis=-1, keepdims=True) + NORM_EPS) * sg_ref[...]
        o_ref[...] = (o * (1.0 - lam_init) * gate).astype(o_ref.dtype)

    pl.when(qi == 0)(lambda: attend(n_ctx))
    pl.when(qi != 0)(lambda: attend(n_all))


def _diff_branch(pb, lam_p, subln_g, cos, sin, n_ctx, lam_init):
    B, N, _ = pb.shape
    blk = lambda c0: c0 // DIFF_VDIM
    return pl.pallas_call(
        functools.partial(_diff_kernel, n_ctx=n_ctx, n_all=N, lam_init=lam_init),
        grid=(B, DIFF_HEADS, N // TOK),
        in_specs=[
            pl.BlockSpec((None, TOK, DIFF_VDIM), lambda b, h, i: (b, i, blk(B_DF_Q) + h)),
            pl.BlockSpec((None, N, DIFF_VDIM), lambda b, h, i: (b, 0, blk(B_DF_K) + h)),
            pl.BlockSpec((None, N, DIFF_VDIM), lambda b, h, i: (b, 0, blk(B_DF_V) + h)),
            pl.BlockSpec((None, TOK, DIFF_VDIM), lambda b, h, i: (b, i, blk(B_DF_G) + h)),
            pl.BlockSpec((N, DIFF_VDIM), lambda b, h, i: (0, 0)),
            pl.BlockSpec((N, DIFF_VDIM), lambda b, h, i: (0, 0)),
            pl.BlockSpec((4, DIFF_HALF), lambda b, h, i: (0, 0)),
            pl.BlockSpec((1, DIFF_VDIM), lambda b, h, i: (0, 0)),
        ],
        out_specs=pl.BlockSpec((None, TOK, DIFF_VDIM), lambda b, h, i: (b, i, h)),
        out_shape=jax.ShapeDtypeStruct((B, N, BRANCH_W), BF16),
        scratch_shapes=[pltpu.VMEM((N, DIFF_VDIM), BF16), pltpu.VMEM((DIFF_VDIM, N), BF16)],
        compiler_params=_cparams(("arbitrary",) * 3),
        name="diff_branch",
    )(pb, pb, pb, pb, cos, sin, lam_p, subln_g.reshape(1, DIFF_VDIM))


XF_T = 128
SCAN_T = 64
SCAN_Q = RWKV_HEAD // 4
KEYED = ("w0", "kka0", "kd0", "w1", "kka1", "kd1", "r", "kk")
KEY_GROUPS = 4


def _rwkv_prep_kernel(r_ref, k_ref, v_ref, lora_ref, wup_hi_ref, wup_lo_ref, aup_hi_ref, aup_lo_ref, w0_ref, a0_ref,
                      kk_g_ref, ka_ref, rk_ref, ones_ref, *refs):
    out = dict(zip(KEYED, refs[:len(KEYED)]))
    zv_ref, bonus_ref, y_ref = refs[len(KEYED):]
    n_b = r_ref.shape[0]
    slot = {name: j * n_b for j, name in enumerate(KEYED + ("v",))}
    copies = LANES // (n_b * RWKV_HEADS)
    group = pl.program_id(1)

    @pl.when(group == 0)
    def _():
        _rwkv_prep_tile(r_ref, k_ref, v_ref, lora_ref, wup_hi_ref, wup_lo_ref, aup_hi_ref, aup_lo_ref, w0_ref, a0_ref,
                        kk_g_ref, ka_ref, rk_ref, ones_ref, bonus_ref, y_ref, slot)
        for val in range(SCAN_Q):
            rows = [y_ref[slot["v"] + b, pl.ds(q * SCAN_Q + val, RWKV_HEADS, stride=RWKV_HEAD), :]
                    for q in range(copies) for b in range(n_b)]
            zv_ref[:, val, :] = jnp.concatenate(rows, axis=0).T

    for g in range(KEY_GROUPS):
        @pl.when(group == g)
        def _(g=g):
            for name in KEYED:
                for j in range(RWKV_HEAD // KEY_GROUPS):
                    key = g * (RWKV_HEAD // KEY_GROUPS) + j
                    rows = [y_ref[slot[name] + b, pl.ds(key, RWKV_HEADS, stride=RWKV_HEAD), :] for b in range(n_b)]
                    out[name][j] = jnp.concatenate(rows * copies, axis=0).T


def _rwkv_prep_tile(r_ref, k_ref, v_ref, lora_ref, wup_hi_ref, wup_lo_ref, aup_hi_ref, aup_lo_ref, w0_ref, a0_ref,
                    kk_g_ref, ka_ref, rk_ref, ones_ref, bonus_ref, y_ref, slot):
    n_b = r_ref.shape[0]
    ones = ones_ref[...]
    for b in range(n_b):
        r = r_ref[b].astype(F32)
        k = k_ref[b].astype(F32)
        v = v_ref[b].astype(F32)
        lora = lora_ref[b]
        wl = jnp.tanh(lora)
        kk = k * kk_g_ref[...]
        kk = kk * lax.rsqrt(jnp.maximum(_split_dot(kk * kk, ones), 1e-24))
        y_ref[slot["r"] + b] = r.T
        y_ref[slot["kk"] + b] = kk.T
        y_ref[slot["v"] + b] = v.T
        bsum = jnp.zeros_like(r)
        for e in range(2):
            w_pre = w0_ref[e:e + 1, :] + _split_dot(wl, wup_hi_ref[e], wup_lo_ref[e])
            y_ref[slot[f"w{e}"] + b] = jnp.exp(-jax.nn.sigmoid(w_pre) * math.exp(-0.5)).T
            a = jax.nn.sigmoid(a0_ref[e:e + 1, :] + _split_dot(lora, aup_hi_ref[e], aup_lo_ref[e]))
            kd = k * (1.0 + (a - 1.0) * ka_ref[...])
            y_ref[slot[f"kd{e}"] + b] = kd.T
            y_ref[slot[f"kka{e}"] + b] = (kk * a).T
            bsum = bsum + kd
        bonus_ref[b] = _split_dot(r * bsum * rk_ref[...], ones) * v


def _rwkv_prep(pa, lora, wup, aup, w0, a0, k_k, k_a, r_k, ones):
    B, N, _ = pa.shape
    blk = lambda c0: c0 // BRANCH_W
    tok_spec = lambda c0: pl.BlockSpec((B, XF_T, BRANCH_W), lambda i, g: (0, i, blk(c0)))
    const = lambda shape: pl.BlockSpec(shape, lambda i, g: (0,) * len(shape), pipeline_mode=pl.Buffered(1))
    split = lambda t: (t.astype(BF16), (t - t.astype(BF16).astype(F32)).astype(BF16))
    wup_hi, wup_lo = split(wup)
    aup_hi, aup_lo = split(aup)
    keyed_shape = jax.ShapeDtypeStruct((N // XF_T, RWKV_HEAD, XF_T, LANES), F32)
    keyed_spec = pl.BlockSpec((None, RWKV_HEAD // KEY_GROUPS, XF_T, LANES), lambda i, g: (i, g, 0, 0))
    return pl.pallas_call(
        _rwkv_prep_kernel,
        grid=(N // XF_T, KEY_GROUPS),
        in_specs=[
            tok_spec(A_RK_R), tok_spec(A_RK_K), tok_spec(A_RK_V),
            pl.BlockSpec((B, XF_T, LORA_PAD), lambda i, g: (0, i, 0)),
            const((2, LORA_PAD, BRANCH_W)), const((2, LORA_PAD, BRANCH_W)),
            const((2, LORA_PAD, BRANCH_W)), const((2, LORA_PAD, BRANCH_W)),
            const((2, BRANCH_W)), const((2, BRANCH_W)),
            const((1, BRANCH_W)), const((1, BRANCH_W)), const((1, BRANCH_W)),
            const((BRANCH_W, BRANCH_W)),
        ],
        out_specs=[keyed_spec] * len(KEYED) + [
            pl.BlockSpec((XF_T, SCAN_Q, LANES), lambda i, g: (i, 0, 0)),
            pl.BlockSpec((B, XF_T, BRANCH_W), lambda i, g: (0, i, 0))],
        out_shape=[keyed_shape] * len(KEYED) + [
            jax.ShapeDtypeStruct((N, SCAN_Q, LANES), F32),
            jax.ShapeDtypeStruct((B, N, BRANCH_W), F32)],
        scratch_shapes=[pltpu.VMEM(((len(KEYED) + 1) * B, BRANCH_W, XF_T), F32)],
        compiler_params=_cparams(("arbitrary", "arbitrary")),
        name="rwkv_prep",
    )(pa, pa, pa, lora, wup_hi, wup_lo, aup_hi, aup_lo, w0, a0,
      k_k.reshape(1, -1), k_a.reshape(1, -1), r_k.reshape(1, -1), ones)


def _rwkv_scan_kernel(wf, kkaf, kdf, wb, kkab, kdb, rf, kkf, rb, kkb, kkn_f, kkn_b, vf, vb,
                      of_ref, ob_ref, ckf, ckb, sf_ref, sb_ref, saf_ref, sab_ref):
    @pl.when(pl.program_id(0) == 0)
    def _():
        for ref in (sf_ref, sb_ref, saf_ref, sab_ref):
            ref[...] = jnp.zeros_like(ref)

    ckf[:, 0:SCAN_T, :] = kkf[...]
    ckf[:, SCAN_T:, :] = kkn_f[...]
    ckb[:, 0:SUBLANES, :] = kkn_b[...]
    ckb[:, SUBLANES:, :] = kkb[...]

    def chain(s_ref, sa, v_t, w, kka, kd, r, kk_next, t, t_next):
        o = jnp.zeros((SCAN_Q, LANES), F32)
        nsa = jnp.zeros((SCAN_Q, LANES), F32)
        row = pl.ds(t, 1)
        row_next = pl.ds(t_next, 1)
        for k in range(RWKV_HEAD):
            s = s_ref[k] * w[k, row, :] + sa * kka[k, row, :] + v_t * kd[k, row, :]
            s_ref[k] = s
            o = o + s * r[k, row, :]
            nsa = nsa + s * kk_next[k, row_next, :]
        return o, -nsa

    def step(t, carry):
        sa_f, sa_b = carry
        tb = SCAN_T - 1 - t
        o_f, sa_f = chain(sf_ref, sa_f, vf[t], wf, kkaf, kdf, rf, ckf, t, t + 1)
        o_b, sa_b = chain(sb_ref, sa_b, vb[tb], wb, kkab, kdb, rb, ckb, tb, tb + SUBLANES - 1)
        of_ref[t] = o_f
        ob_ref[tb] = o_b
        return sa_f, sa_b

    saf_ref[...], sab_ref[...] = lax.fori_loop(0, SCAN_T, step, (saf_ref[...], sab_ref[...]))


def _rwkv_scan(z, zv, n_ctx):
    N = zv.shape[0]
    nb, nc = N // SCAN_T, n_ctx // SCAN_T
    per_tile = XF_T // SCAN_T

    def mirror(i):
        return jnp.where(i < nc, nc - 1 - i, nb + nc - 1 - i)

    def tok_f(i):
        return jnp.minimum((i + 1) * SCAN_T, N - SUBLANES)

    def tok_b(i):
        return mirror(jnp.minimum(i + 1, nb - 1)) * SCAN_T + SCAN_T - 1

    def keyed(block_of):
        return pl.BlockSpec((None, RWKV_HEAD, SCAN_T, LANES),
                            lambda i: (block_of(i) // per_tile, 0, block_of(i) % per_tile, 0))

    def group(tok_of):
        return pl.BlockSpec((None, RWKV_HEAD, SUBLANES, LANES),
                            lambda i: (tok_of(i) // XF_T, 0, (tok_of(i) % XF_T) // SUBLANES, 0))

    fwd, bwd = keyed(lambda i: i), keyed(mirror)
    vf = pl.BlockSpec((SCAN_T, SCAN_Q, LANES), lambda i: (i, 0, 0))
    vb = pl.BlockSpec((SCAN_T, SCAN_Q, LANES), lambda i: (mirror(i), 0, 0))
    out = jax.ShapeDtypeStruct((N, SCAN_Q, LANES), F32)
    state = pltpu.VMEM((RWKV_HEAD, SCAN_Q, LANES), F32)
    return pl.pallas_call(
        _rwkv_scan_kernel,
        grid=(nb,),
        in_specs=[fwd] * 3 + [bwd] * 3 + [fwd, fwd, bwd, bwd, group(tok_f), group(tok_b), vf, vb],
        out_specs=[vf, vb],
        out_shape=[out, out],
        scratch_shapes=[pltpu.VMEM((RWKV_HEAD, SCAN_T + SUBLANES, LANES), F32)] * 2 + [state, state] + [
            pltpu.VMEM((SCAN_Q, LANES), F32)] * 2,
        compiler_params=_cparams(("arbitrary",)),
        name="rwkv_scan",
    )(z["w0"], z["kka0"], z["kd0"], z["w1"], z["kka1"], z["kd1"], z["r"], z["kk"], z["r"], z["kk"],
      z["kk"], z["kk"], zv, zv)


def _rwkv_post_kernel(of_ref, ob_ref, bonus_ref, gate_ref, g_ref, b_ref, ones_ref, o_ref, y_ref):
    n_b = bonus_ref.shape[0]
    copies = LANES // (n_b * RWKV_HEADS)
    for val in range(SCAN_Q):
        t = (of_ref[:, val, :] + ob_ref[:, val, :]).T
        for q in range(copies):
            for b in range(n_b):
                lo = (q * n_b + b) * RWKV_HEADS
                y_ref[b, pl.ds(q * SCAN_Q + val, RWKV_HEADS, stride=RWKV_HEAD), :] = t[lo:lo + RWKV_HEADS]
    ones = ones_ref[...]
    for b in range(n_b):
        y = y_ref[b].T
        mu = _split_dot(y, ones) * (1.0 / RWKV_HEAD)
        d = y - mu
        var = _split_dot(d * d, ones) * (1.0 / RWKV_HEAD)
        yn = d * lax.rsqrt(var + RWKV_GN_EPS) * g_ref[...] + b_ref[...]
        o_ref[b] = ((yn + bonus_ref[b]) * _silu(gate_ref[b].astype(F32))).astype(o_ref.dtype)


def _rwkv_post(o_f, o_b, bonus, pb, gn_g, gn_b, ones):
    B, N, _ = bonus.shape
    nat = pl.BlockSpec((B, XF_T, BRANCH_W), lambda i: (0, i, 0))
    scan = pl.BlockSpec((XF_T, SCAN_Q, LANES), lambda i: (i, 0, 0))
    const = lambda shape: pl.BlockSpec(shape, lambda i: (0, 0))
    return pl.pallas_call(
        _rwkv_post_kernel,
        grid=(N // XF_T,),
        in_specs=[scan, scan, nat,
                  pl.BlockSpec((B, XF_T, BRANCH_W), lambda i: (0, i, B_RK_G // BRANCH_W)),
                  const((1, BRANCH_W)), const((1, BRANCH_W)), const((BRANCH_W, BRANCH_W))],
        out_specs=nat,
        out_shape=jax.ShapeDtypeStruct((B, N, BRANCH_W), BF16),
        scratch_shapes=[pltpu.VMEM((B, BRANCH_W, XF_T), F32)],
        compiler_params=_cparams(("arbitrary",)),
        name="rwkv_post",
    )(o_f, o_b, bonus, pb, gn_g.reshape(1, -1), gn_b.reshape(1, -1), ones)


def _rwkv_branch(pa, pb, lora, li, p, ones, n_ctx):
    B, N, _ = pa.shape
    assert LANES % (B * RWKV_HEADS) == 0 and LANES // (B * RWKV_HEADS) * SCAN_Q == RWKV_HEAD
    assert n_ctx % SCAN_T == 0 and N % XF_T == 0 and XF_T % SCAN_T == 0
    outs = _rwkv_prep(pa, lora, p["wup"][li], p["aup"][li], p["rwkv_w0"][li], p["rwkv_a0"][li],
                      p["rwkv_k_k"][li], p["rwkv_k_a"][li], p["rwkv_r_k"][li], ones)
    z = dict(zip(KEYED, outs[:len(KEYED)]))
    zv, bonus = outs[len(KEYED):]
    o_f, o_b = _rwkv_scan(z, zv, n_ctx)
    return _rwkv_post(o_f, o_b, bonus, pb, p["rwkv_gn_g"][li], p["rwkv_gn_b"][li], ones)


def _merge_kernel(bc_ref, br_ref, bd_ref, bq_ref, lg_ref, wb_ref, bg_ref, wo_ref, x_ref, gp_ref, gt_ref,
                  sh_ref, sc_ref, gn_ref, o_ref, h_ref):
    merged = jnp.zeros((TOK, D_MODEL), F32)
    for j, b_ref in enumerate((bc_ref, br_ref, bd_ref, bq_ref)):
        pj = jnp.dot(b_ref[...], wb_ref[j], preferred_element_type=F32)
        gj = jax.nn.sigmoid(lg_ref[:, j * D_MODEL:(j + 1) * D_MODEL].astype(F32) + bg_ref[j:j + 1, :])
        merged = merged + pj * gj
    y = jnp.dot(merged.astype(BF16), wo_ref[...], preferred_element_type=F32)
    y = y * lax.rsqrt(jnp.mean(y * y, axis=-1, keepdims=True) + NORM_EPS) * gp_ref[...]
    x = x_ref[...] + gt_ref[...] * y
    o_ref[...] = x
    hn = x * lax.rsqrt(jnp.mean(x * x, axis=-1, keepdims=True) + NORM_EPS) * gn_ref[...]
    h_ref[...] = (hn * (1.0 + sc_ref[...]) + sh_ref[...]).astype(h_ref.dtype)


def _merge(branches, pc, w_branch, b_gate, w_out, li, xa, g_post, mod_l, mod_next, g_next):
    B, N, _ = xa.shape
    row = _mod_row(B)
    bspec = pl.BlockSpec((None, TOK, BRANCH_W), lambda b, i: (b, i, 0))
    xspec = pl.BlockSpec((None, TOK, D_MODEL), lambda b, i: (b, i, 0))
    mod_spec = lambda part: pl.BlockSpec((None, 1, D_MODEL), lambda b, i: row(b, i)[:2] + (part,))
    vec_spec = pl.BlockSpec((1, D_MODEL), lambda b, i: (0, 0))
    once = pl.Buffered(1)
    return pl.pallas_call(
        _merge_kernel,
        grid=(B, N // TOK),
        in_specs=[bspec] * 4 + [
            pl.BlockSpec((None, TOK, N_BRANCH * D_MODEL), lambda b, i: (b, i, 0)),
            pl.BlockSpec((None, N_BRANCH, BRANCH_W, D_MODEL), lambda b, i: (li, 0, 0, 0), pipeline_mode=once),
            pl.BlockSpec((None, N_BRANCH, D_MODEL), lambda b, i: (li, 0, 0)),
            pl.BlockSpec((None, D_MODEL, D_MODEL), lambda b, i: (li, 0, 0), pipeline_mode=once),
            xspec, vec_spec, mod_spec(2), mod_spec(0), mod_spec(1), vec_spec,
        ],
        out_specs=[xspec, xspec],
        out_shape=[jax.ShapeDtypeStruct((B, N, D_MODEL), F32), jax.ShapeDtypeStruct((B, N, D_MODEL), BF16)],
        compiler_params=_cparams(("arbitrary", "arbitrary")),
        name="merge",
    )(*branches, pc, w_branch, b_gate, w_out, xa, g_post.reshape(1, -1), mod_l, mod_next, mod_next,
      g_next.reshape(1, -1))


def kernel(x, c, ctx, c_ctx, norm_pre_g, norm_post_g, w_mod, b_mod, w_in, conv_w, conv_b, conv_ln_g, conv_ln_b,
           rwkv_w0, rwkv_w_up, rwkv_a0, rwkv_a_up, rwkv_k_k, rwkv_k_a, rwkv_r_k, rwkv_gn_g, rwkv_gn_b,
           diff_lam, diff_subln_g, gqa_qk_norm_g, w_branch, b_gate, w_out):
    B, n_lat, _ = x.shape
    n_ctx = ctx.shape[1]
    depth = w_in.shape[0]
    N = n_ctx + n_lat
    assert n_ctx == TOK and n_lat % TOK == 0 and B + 1 <= MOD_ROWS and w_in.shape[-1] == D_IN

    w_in_b = w_in.astype(BF16)
    w_branch_b = w_branch.astype(BF16)
    w_out_b = w_out.astype(BF16)
    zrows = lambda t, lo: jnp.pad(t, ((0, 0), (0, 0), (lo, LORA_PAD - lo - t.shape[2]), (0, 0)))
    params = dict(wup=zrows(rwkv_w_up, 0), aup=zrows(rwkv_a_up, DECAY_LORA), rwkv_w0=rwkv_w0, rwkv_a0=rwkv_a0,
                  rwkv_k_k=rwkv_k_k, rwkv_k_a=rwkv_k_a, rwkv_r_k=rwkv_r_k.reshape(depth, BRANCH_W),
                  rwkv_gn_g=rwkv_gn_g, rwkv_gn_b=rwkv_gn_b)
    head = np.arange(BRANCH_W) // RWKV_HEAD
    ones = jnp.asarray(head[:, None] == head[None, :], BF16)
    cos_d, sin_d = _rope_tables(n_ctx, n_lat, DIFF_HALF, 2)
    cos_g, sin_g = _rope_tables(n_ctx, n_lat, GQA_HEAD, 1)

    cvec = jnp.concatenate([c, c_ctx[None, :], jnp.zeros((MOD_ROWS - B - 1, D_MODEL), c.dtype)], axis=0)
    mod = _modulation(cvec, w_mod, b_mod)
    xa = jnp.concatenate([ctx, x], axis=1)
    tm = 2176 if (B * N) % 2176 == 0 else TOK
    tm_s = 1088 if (B * N) % 1088 == 0 else TOK

    h = _prenorm(xa, mod[0], norm_pre_g[0])
    for li in range(depth):
        h = h.reshape(B * N, D_MODEL)
        pa = _project(h, w_in_b, li, O_A, O_LORA - O_A, 1024, BF16, tm, (B, N, O_LORA - O_A))
        pb = _project(h, w_in_b, li, O_B, O_C - O_B, 768, BF16, tm_s, (B, N, O_C - O_B))
        pc = _project(h, w_in_b, li, O_C, D_IN - O_C, 1024, BF16, tm_s, (B, N, D_IN - O_C))
        lora = _project(h, w_in_b, li, O_LORA, LORA_PAD, LORA_PAD, F32, tm, (B, N, LORA_PAD))
        lam_init = 0.8 - 0.6 * math.exp(-0.3 * li)
        br_conv = _conv_branch(pa, conv_w[li], conv_b[li], conv_ln_g[li], conv_ln_b[li])
        br_rwkv = _rwkv_branch(pa, pb, lora, li, params, ones, n_ctx)
        br_diff = _diff_branch(pb, diff_lam[li], diff_subln_g[li], cos_d, sin_d, n_ctx, lam_init)
        br_gqa = _gqa_branch(pb, gqa_qk_norm_g[li], cos_g, sin_g, n_ctx)
        nxt = min(li + 1, depth - 1)
        xa, h = _merge((br_conv, br_rwkv, br_diff, br_gqa), pc, w_branch_b, b_gate, w_out_b, li, xa,
                       norm_post_g[li], mod[li], mod[nxt], norm_pre_g[nxt])
    return xa[:, n_ctx:]
```

```python
import functools
import math

import jax
import jax.numpy as jnp
import numpy as np
from jax import lax
from jax.experimental import pallas as pl
from jax.experimental.pallas import tpu as pltpu

F32 = jnp.float32
BF16 = jnp.bfloat16

D_MODEL = 2048
BRANCH_W = D_MODEL // 2
N_BRANCH = 4
GRID_W = 64
CONV_K = 31
CONV_HALO = 16
RWKV_HEAD = 64
RWKV_HEADS = BRANCH_W // RWKV_HEAD
DECAY_LORA = 96
AAA_LORA = 96
LORA_PAD = 256
RWKV_GN_EPS = 64e-5
DIFF_HALF = 64
DIFF_VDIM = 2 * DIFF_HALF
DIFF_HEADS = BRANCH_W // DIFF_VDIM
GQA_HEAD = 128
GQA_HEADS = BRANCH_W // GQA_HEAD
GQA_KV_HEADS = GQA_HEADS // 4
GQA_REP = GQA_HEADS // GQA_KV_HEADS
ROPE_THETA = 10000.0
NORM_EPS = 1e-6

TOK = 256
LANES = 128
SUBLANES = 8
MOD_ROWS = 8
VMEM_LIMIT = 60 * 1024 * 1024

O_A = 0
O_LORA = 6 * BRANCH_W
O_B = O_LORA + DECAY_LORA + AAA_LORA
O_C = O_B + 6 * BRANCH_W + 2 * GQA_KV_HEADS * GQA_HEAD + BRANCH_W
D_IN = O_C + N_BRANCH * D_MODEL
A_CONV, A_RK_R, A_RK_K, A_RK_V = 0, 3 * BRANCH_W, 4 * BRANCH_W, 5 * BRANCH_W
B_RK_G, B_DF_Q, B_DF_K, B_DF_V, B_DF_G, B_GQ_Q = (j * BRANCH_W for j in range(6))
B_GQ_K = 6 * BRANCH_W
B_GQ_V = B_GQ_K + GQA_KV_HEADS * GQA_HEAD
B_GQ_G = B_GQ_V + GQA_KV_HEADS * GQA_HEAD


def _cparams(sem):
    return pltpu.CompilerParams(dimension_semantics=sem, vmem_limit_bytes=VMEM_LIMIT)


def _silu(t):
    return t * jax.nn.sigmoid(t)


def _split_dot(a, b_hi, b_lo=None):
    a_hi = a.astype(BF16)
    a_lo = (a - a_hi.astype(F32)).astype(BF16)
    out = jnp.dot(a_hi, b_hi, preferred_element_type=F32) + jnp.dot(a_lo, b_hi, preferred_element_type=F32)
    if b_lo is not None:
        out = out + jnp.dot(a_hi, b_lo, preferred_element_type=F32)
    return out


def _mod_kernel(c_ref, w_ref, b_ref, o_ref):
    cv = c_ref[...]
    s = _silu(cv).astype(BF16)
    o_ref[...] = jnp.dot(s, w_ref[...].astype(BF16), preferred_element_type=F32) + b_ref[...]


def _modulation(cvec, w_mod, b_mod):
    L = w_mod.shape[0]
    tn = 1536
    out = pl.pallas_call(
        _mod_kernel,
        grid=(L, 3 * D_MODEL // tn),
        in_specs=[
            pl.BlockSpec((MOD_ROWS, D_MODEL), lambda l, j: (0, 0)),
            pl.BlockSpec((None, D_MODEL, tn), lambda l, j: (l, 0, j)),
            pl.BlockSpec((None, 1, tn), lambda l, j: (l, 0, j)),
        ],
        out_specs=pl.BlockSpec((None, MOD_ROWS, tn), lambda l, j: (l, 0, j)),
        out_shape=jax.ShapeDtypeStruct((L, MOD_ROWS, 3 * D_MODEL), F32),
        compiler_params=_cparams(("arbitrary", "arbitrary")),
        name="modulation",
    )(cvec, w_mod, b_mod.reshape(L, 1, 3 * D_MODEL))
    return out.reshape(L, MOD_ROWS, 1, 3 * D_MODEL)


def _mod_row(batch):
    return lambda b, i: (jnp.where(i == 0, batch, b), 0, 0)


def _prenorm_kernel(x_ref, sh_ref, sc_ref, g_ref, h_ref):
    x = x_ref[...]
    y = x * lax.rsqrt(jnp.mean(x * x, axis=-1, keepdims=True) + NORM_EPS) * g_ref[...]
    h_ref[...] = (y * (1.0 + sc_ref[...]) + sh_ref[...]).astype(h_ref.dtype)


def _prenorm(xa, mod_l, g):
    B, N, _ = xa.shape
    row = _mod_row(B)
    return pl.pallas_call(
        _prenorm_kernel,
        grid=(B, N // TOK),
        in_specs=[
            pl.BlockSpec((None, TOK, D_MODEL), lambda b, i: (b, i, 0)),
            pl.BlockSpec((None, 1, D_MODEL), lambda b, i: row(b, i)[:2] + (0,)),
            pl.BlockSpec((None, 1, D_MODEL), lambda b, i: row(b, i)[:2] + (1,)),
            pl.BlockSpec((1, D_MODEL), lambda b, i: (0, 0)),
        ],
        out_specs=pl.BlockSpec((None, TOK, D_MODEL), lambda b, i: (b, i, 0)),
        out_shape=jax.ShapeDtypeStruct((B, N, D_MODEL), BF16),
        compiler_params=_cparams(("arbitrary", "arbitrary")),
        name="prenorm",
    )(xa, mod_l, mod_l, g.reshape(1, D_MODEL))


def _mm_kernel(a_ref, w_ref, o_ref):
    o_ref[...] = jnp.dot(a_ref[...], w_ref[...], preferred_element_type=F32).astype(o_ref.dtype)


def _mm_shift_kernel(a_ref, w_lo_ref, w_hi_ref, o_ref, w_ref, *, shift):
    @pl.when(pl.program_id(1) == 0)
    def _():
        w_ref[...] = jnp.concatenate([w_lo_ref[:, shift:], w_hi_ref[:, :shift]], axis=1)

    o_ref[...] = jnp.dot(a_ref[...], w_ref[...], preferred_element_type=F32).astype(o_ref.dtype)


def _project(h, w_all, li, col0, ncols, tn, out_dtype, tm, shape3):
    M, K = h.shape
    first, shift = divmod(col0, tn)
    w_spec = lambda off: pl.BlockSpec((None, K, tn), lambda j, i: (li, 0, first + off + j))
    if shift == 0:
        body, w_specs, scratch = _mm_kernel, [w_spec(0)], []
    else:
        body = functools.partial(_mm_shift_kernel, shift=shift)
        w_specs, scratch = [w_spec(0), w_spec(1)], [pltpu.VMEM((K, tn), w_all.dtype)]
    out = pl.pallas_call(
        body,
        grid=(ncols // tn, M // tm),
        in_specs=[pl.BlockSpec((tm, K), lambda j, i: (i, 0))] + w_specs,
        out_specs=pl.BlockSpec((tm, tn), lambda j, i: (i, j)),
        out_shape=jax.ShapeDtypeStruct((M, ncols), out_dtype),
        scratch_shapes=scratch,
        compiler_params=_cparams(("arbitrary", "arbitrary")),
        name="in_proj",
    )(h, *([w_all] * len(w_specs)))
    return out.reshape(shape3)


def _conv_kernel(cur_ref, gate_ref, prev_ref, next_ref, cw_ref, cb_ref, lg_ref, lb_ref, o_ref, buf_ref, *, n_tiles):
    i = pl.program_id(1)

    def glu(t):
        return t[:, :BRANCH_W].astype(F32) * jax.nn.sigmoid(t[:, BRANCH_W:2 * BRANCH_W].astype(F32))

    has_prev = jnp.logical_and(i != 0, i != 1)
    has_next = jnp.logical_and(i != 0, i != n_tiles - 1)
    cur = cur_ref[...]
    buf_ref[0:CONV_HALO, :] = jnp.where(has_prev, glu(prev_ref[...]), 0.0)
    buf_ref[CONV_HALO:CONV_HALO + TOK, :] = glu(cur)
    buf_ref[CONV_HALO + TOK:, :] = jnp.where(has_next, glu(next_ref[...]), 0.0)
    base = CONV_HALO - CONV_K // 2
    acc = jnp.broadcast_to(cb_ref[...], (TOK, BRANCH_W))
    for r in range(SUBLANES):
        part = None
        for j in range(CONV_K):
            if (base + j) % SUBLANES == r:
                lo = base + j - r
                term = cw_ref[j:j + 1, :] * buf_ref[lo:lo + TOK + SUBLANES, :]
                part = term if part is None else part + term
        if part is not None:
            acc = acc + part[r:r + TOK]
    mu = jnp.mean(acc, axis=-1, keepdims=True)
    d = acc - mu
    var = jnp.mean(d * d, axis=-1, keepdims=True)
    y = d * lax.rsqrt(var + 1e-5) * lg_ref[...] + lb_ref[...]
    o_ref[...] = (_silu(y) * _silu(gate_ref[...].astype(F32))).astype(o_ref.dtype)


def _conv_branch(pa, conv_w, conv_b, ln_g, ln_b):
    B, N, _ = pa.shape
    nt = N // TOK
    hpt = TOK // CONV_HALO
    last = N // CONV_HALO - 1
    cblk = A_CONV // (2 * BRANCH_W)
    return pl.pallas_call(
        functools.partial(_conv_kernel, n_tiles=nt),
        grid=(B, nt),
        in_specs=[
            pl.BlockSpec((None, TOK, 2 * BRANCH_W), lambda b, i: (b, i, cblk)),
            pl.BlockSpec((None, TOK, BRANCH_W), lambda b, i: (b, i, 2 * cblk + 2)),
            pl.BlockSpec((None, CONV_HALO, 2 * BRANCH_W), lambda b, i: (b, jnp.maximum(i * hpt - 1, 0), cblk)),
            pl.BlockSpec((None, CONV_HALO, 2 * BRANCH_W), lambda b, i: (b, jnp.minimum((i + 1) * hpt, last), cblk)),
            pl.BlockSpec((CONV_K, BRANCH_W), lambda b, i: (0, 0)),
            pl.BlockSpec((1, BRANCH_W), lambda b, i: (0, 0)),
            pl.BlockSpec((1, BRANCH_W), lambda b, i: (0, 0)),
            pl.BlockSpec((1, BRANCH_W), lambda b, i: (0, 0)),
        ],
        out_specs=pl.BlockSpec((None, TOK, BRANCH_W), lambda b, i: (b, i, 0)),
        out_shape=jax.ShapeDtypeStruct((B, N, BRANCH_W), BF16),
        scratch_shapes=[pltpu.VMEM((TOK + 2 * CONV_HALO, BRANCH_W), F32)],
        compiler_params=_cparams(("arbitrary", "arbitrary")),
        name="conv_branch",
    )(pa, pa, pa, pa, conv_w, conv_b.reshape(1, -1), ln_g.reshape(1, -1), ln_b.reshape(1, -1))


def _rope_tables(n_ctx, n_lat, dim, reps):
    half = dim // 2
    t = np.arange(n_lat)
    inv = ROPE_THETA ** (-np.arange(0, half, 2, dtype=np.float64) / half)
    ang = np.concatenate([(t // GRID_W)[:, None] * inv, (t % GRID_W)[:, None] * inv], axis=-1)
    ang = np.concatenate([np.zeros((n_ctx, half)), ang], axis=0)
    cos = np.repeat(np.cos(ang), 2, axis=-1)
    sin = np.repeat(np.sin(ang), 2, axis=-1) * np.tile(np.array([-1.0, 1.0]), half)
    return (jnp.asarray(np.tile(cos, (1, reps)), F32), jnp.asarray(np.tile(sin, (1, reps)), F32))


def _rope(t, cos, sin_signed):
    lane = lax.broadcasted_iota(jnp.int32, t.shape, 1)
    partner = jnp.where(lane % 2 == 0, pltpu.roll(t, LANES - 1, 1), pltpu.roll(t, 1, 1))
    return t * cos + partner * sin_signed


LOG2_E = 1.4426950408889634


ATTN_CHUNK = 1536


def _scores_t(k, q):
    return lax.dot_general(k, q, (((1,), (1,)), ((), ())), preferred_element_type=F32)


def _online_attention(kp_ref, vt_ref, qs, nk):
    chunks = [(lo, min(ATTN_CHUNK, nk - lo)) for lo in range(0, nk, ATTN_CHUNK)]

    def scores(c):
        lo, n = chunks[c]
        k = kp_ref[lo:lo + n, :]
        return [_scores_t(k, q) for q in qs]

    state = [None] * len(qs)
    s_next = scores(0)
    for c, (lo, n) in enumerate(chunks):
        s_cur = s_next
        if c + 1 < len(chunks):
            s_next = scores(c + 1)
        v = vt_ref[:, lo:lo + n]
        for j, s in enumerate(s_cur):
            m = jnp.max(s, axis=0, keepdims=True)
            if state[j] is not None:
                m_old, l_old, acc_old = state[j]
                m = jnp.maximum(m_old, m)
            e = jnp.exp2(s - m)
            l = jnp.sum(e, axis=0, keepdims=True)
            acc = jnp.dot(v, e.astype(BF16), preferred_element_type=F32)
            if state[j] is not None:
                alpha = jnp.exp2(m_old - m)
                l = l_old * alpha + l
                acc = acc_old * alpha + acc
            state[j] = (m, l, acc)
    return [(acc, l) for _, l, acc in state]


def _gqa_kernel(q_ref, k_ref, v_ref, gate_ref, cos_ref, sin_ref, g_ref, o_ref, kp_ref, vt_ref, *, n_ctx, n_all):
    r = pl.program_id(2)
    qi = pl.program_id(3)

    @pl.when(jnp.logical_and(r == 0, qi == 0))
    def _():
        k = k_ref[...].astype(F32)
        k = k * lax.rsqrt(jnp.mean(k * k, axis=-1, keepdims=True) + NORM_EPS) * g_ref[1:2, :]
        kp_ref[...] = _rope(k, cos_ref[...], sin_ref[...]).astype(BF16)
        vt_ref[...] = v_ref[...].astype(F32).T.astype(BF16)

    rows = pl.ds(pl.multiple_of(qi * TOK, TOK), TOK)
    qs = []
    for j in range(GQA_PAIR):
        q = q_ref[:, j * GQA_HEAD:(j + 1) * GQA_HEAD].astype(F32)
        q = q * lax.rsqrt(jnp.mean(q * q, axis=-1, keepdims=True) + NORM_EPS) * g_ref[0:1, :]
        q = _rope(q, cos_ref[rows, :], sin_ref[rows, :]) * (GQA_HEAD ** -0.5 * LOG2_E)
        qs.append(q.astype(BF16))
    gate = _silu(gate_ref[...].astype(F32))

    def attend(nk):
        o = jnp.concatenate([(acc / l).T for acc, l in _online_attention(kp_ref, vt_ref, qs, nk)], axis=1)
        o_ref[...] = (o * gate).astype(o_ref.dtype)

    pl.when(qi == 0)(lambda: attend(n_ctx))
    pl.when(qi != 0)(lambda: attend(n_all))


GQA_PAIR = 2


def _gqa_branch(pb, qk_g, cos, sin, n_ctx):
    B, N, _ = pb.shape
    blk = lambda c0: c0 // GQA_HEAD
    wide = GQA_PAIR * GQA_HEAD
    steps = GQA_REP // GQA_PAIR
    return pl.pallas_call(
        functools.partial(_gqa_kernel, n_ctx=n_ctx, n_all=N),
        grid=(B, GQA_KV_HEADS, steps, N // TOK),
        in_specs=[
            pl.BlockSpec((None, TOK, wide), lambda b, g, r, i: (b, i, B_GQ_Q // wide + g * steps + r)),
            pl.BlockSpec((None, N, GQA_HEAD), lambda b, g, r, i: (b, 0, blk(B_GQ_K) + g)),
            pl.BlockSpec((None, N, GQA_HEAD), lambda b, g, r, i: (b, 0, blk(B_GQ_V) + g)),
            pl.BlockSpec((None, TOK, wide), lambda b, g, r, i: (b, i, B_GQ_G // wide + g * steps + r)),
            pl.BlockSpec((N, GQA_HEAD), lambda b, g, r, i: (0, 0)),
            pl.BlockSpec((N, GQA_HEAD), lambda b, g, r, i: (0, 0)),
            pl.BlockSpec((2, GQA_HEAD), lambda b, g, r, i: (0, 0)),
        ],
        out_specs=pl.BlockSpec((None, TOK, wide), lambda b, g, r, i: (b, i, g * steps + r)),
        out_shape=jax.ShapeDtypeStruct((B, N, BRANCH_W), BF16),
        scratch_shapes=[pltpu.VMEM((N, GQA_HEAD), BF16), pltpu.VMEM((GQA_HEAD, N), BF16)],
        compiler_params=_cparams(("arbitrary",) * 4),
        name="gqa_branch",
    )(pb, pb, pb, pb, cos, sin, qk_g)


def _diff_kernel(q_ref, k_ref, v_ref, gate_ref, cos_ref, sin_ref, lam_ref, sg_ref, o_ref, kp_ref, vt_ref,
                 *, n_ctx, n_all, lam_init):
    qi = pl.program_id(2)

    @pl.when(qi == 0)
    def _():
        kp_ref[...] = _rope(k_ref[...].astype(F32), cos_ref[...], sin_ref[...]).astype(BF16)
        vt_ref[...] = v_ref[...].astype(F32).T.astype(BF16)

    lp = lam_ref[...]
    lam = (jnp.exp(jnp.sum(lp[0:1] * lp[1:2], axis=-1, keepdims=True))
           - jnp.exp(jnp.sum(lp[2:3] * lp[3:4], axis=-1, keepdims=True)) + lam_init)
    rows = pl.ds(pl.multiple_of(qi * TOK, TOK), TOK)
    q = _rope(q_ref[...].astype(F32), cos_ref[rows, :], sin_ref[rows, :]) * (DIFF_HALF ** -0.5 * LOG2_E)
    lane = lax.broadcasted_iota(jnp.int32, q.shape, 1)
    q1 = jnp.where(lane < DIFF_HALF, q, 0.0).astype(BF16)
    q2 = jnp.where(lane >= DIFF_HALF, q, 0.0).astype(BF16)
    gate = _silu(gate_ref[...].astype(F32))

    def attend(nk):
        (acc1, l1), (acc2, l2) = _online_attention(kp_ref, vt_ref, (q1, q2), nk)
        o = (acc1 / l1 - acc2 * (lam / l2)).T
        o = o * lax.rsqrt(jnp.mean(o * o, axis=-1, keepdims=True) + NORM_EPS) * sg_ref[...]
        o_ref[...] = (o * (1.0 - lam_init) * gate).astype(o_ref.dtype)

    pl.when(qi == 0)(lambda: attend(n_ctx))
    pl.when(qi != 0)(lambda: attend(n_all))


def _diff_branch(pb, lam_p, subln_g, cos, sin, n_ctx, lam_init):
    B, N, _ = pb.shape
    blk = lambda c0: c0 // DIFF_VDIM
    return pl.pallas_call(
        functools.partial(_diff_kernel, n_ctx=n_ctx, n_all=N, lam_init=lam_init),
        grid=(B, DIFF_HEADS, N // TOK),
        in_specs=[
            pl.BlockSpec((None, TOK, DIFF_VDIM), lambda b, h, i: (b, i, blk(B_DF_Q) + h)),
            pl.BlockSpec((None, N, DIFF_VDIM), lambda b, h, i: (b, 0, blk(B_DF_K) + h)),
            pl.BlockSpec((None, N, DIFF_VDIM), lambda b, h, i: (b, 0, blk(B_DF_V) + h)),
            pl.BlockSpec((None, TOK, DIFF_VDIM), lambda b, h, i: (b, i, blk(B_DF_G) + h)),
            pl.BlockSpec((N, DIFF_VDIM), lambda b, h, i: (0, 0)),
            pl.BlockSpec((N, DIFF_VDIM), lambda b, h, i: (0, 0)),
            pl.BlockSpec((4, DIFF_HALF), lambda b, h, i: (0, 0)),
            pl.BlockSpec((1, DIFF_VDIM), lambda b, h, i: (0, 0)),
        ],
        out_specs=pl.BlockSpec((None, TOK, DIFF_VDIM), lambda b, h, i: (b, i, h)),
        out_shape=jax.ShapeDtypeStruct((B, N, BRANCH_W), BF16),
        scratch_shapes=[pltpu.VMEM((N, DIFF_VDIM), BF16), pltpu.VMEM((DIFF_VDIM, N), BF16)],
        compiler_params=_cparams(("arbitrary",) * 3),
        name="diff_branch",
    )(pb, pb, pb, pb, cos, sin, lam_p, subln_g.reshape(1, DIFF_VDIM))


XF_T = 128
SCAN_T = 64
SCAN_Q = RWKV_HEAD // 4
KEYED = ("w0", "kka0", "kd0", "w1", "kka1", "kd1", "r", "kk")
KEY_GROUPS = 4


def _rwkv_prep_kernel(r_ref, k_ref, v_ref, lora_ref, wup_hi_ref, wup_lo_ref, aup_hi_ref, aup_lo_ref, w0_ref, a0_ref,
                      kk_g_ref, ka_ref, rk_ref, ones_ref, *refs):
    out = dict(zip(KEYED, refs[:len(KEYED)]))
    zv_ref, bonus_ref, y_ref = refs[len(KEYED):]
    n_b = r_ref.shape[0]
    slot = {name: j * n_b for j, name in enumerate(KEYED + ("v",))}
    copies = LANES // (n_b * RWKV_HEADS)
    group = pl.program_id(1)

    @pl.when(group == 0)
    def _():
        _rwkv_prep_tile(r_ref, k_ref, v_ref, lora_ref, wup_hi_ref, wup_lo_ref, aup_hi_ref, aup_lo_ref, w0_ref, a0_ref,
                        kk_g_ref, ka_ref, rk_ref, ones_ref, bonus_ref, y_ref, slot)
        for val in range(SCAN_Q):
            rows = [y_ref[slot["v"] + b, pl.ds(q * SCAN_Q + val, RWKV_HEADS, stride=RWKV_HEAD), :]
                    for q in range(copies) for b in range(n_b)]
            zv_ref[:, val, :] = jnp.concatenate(rows, axis=0).T

    for g in range(KEY_GROUPS):
        @pl.when(group == g)
        def _(g=g):
            for name in KEYED:
                for j in range(RWKV_HEAD // KEY_GROUPS):
                    key = g * (RWKV_HEAD // KEY_GROUPS) + j
                    rows = [y_ref[slot[name] + b, pl.ds(key, RWKV_HEADS, stride=RWKV_HEAD), :] for b in range(n_b)]
                    out[name][j] = jnp.concatenate(rows * copies, axis=0).T


def _rwkv_prep_tile(r_ref, k_ref, v_ref, lora_ref, wup_hi_ref, wup_lo_ref, aup_hi_ref, aup_lo_ref, w0_ref, a0_ref,
                    kk_g_ref, ka_ref, rk_ref, ones_ref, bonus_ref, y_ref, slot):
    n_b = r_ref.shape[0]
    ones = ones_ref[...]
    for b in range(n_b):
        r = r_ref[b].astype(F32)
        k = k_ref[b].astype(F32)
        v = v_ref[b].astype(F32)
        lora = lora_ref[b]
        wl = jnp.tanh(lora)
        kk = k * kk_g_ref[...]
        kk = kk * lax.rsqrt(jnp.maximum(_split_dot(kk * kk, ones), 1e-24))
        y_ref[slot["r"] + b] = r.T
        y_ref[slot["kk"] + b] = kk.T
        y_ref[slot["v"] + b] = v.T
        bsum = jnp.zeros_like(r)
        for e in range(2):
            w_pre = w0_ref[e:e + 1, :] + _split_dot(wl, wup_hi_ref[e], wup_lo_ref[e])
            y_ref[slot[f"w{e}"] + b] = jnp.exp(-jax.nn.sigmoid(w_pre) * math.exp(-0.5)).T
            a = jax.nn.sigmoid(a0_ref[e:e + 1, :] + _split_dot(lora, aup_hi_ref[e], aup_lo_ref[e]))
            kd = k * (1.0 + (a - 1.0) * ka_ref[...])
            y_ref[slot[f"kd{e}"] + b] = kd.T
            y_ref[slot[f"kka{e}"] + b] = (kk * a).T
            bsum = bsum + kd
        bonus_ref[b] = _split_dot(r * bsum * rk_ref[...], ones) * v


def _rwkv_prep(pa, lora, wup, aup, w0, a0, k_k, k_a, r_k, ones):
    B, N, _ = pa.shape
    blk = lambda c0: c0 // BRANCH_W
    tok_spec = lambda c0: pl.BlockSpec((B, XF_T, BRANCH_W), lambda i, g: (0, i, blk(c0)))
    const = lambda shape: pl.BlockSpec(shape, lambda i, g: (0,) * len(shape), pipeline_mode=pl.Buffered(1))
    split = lambda t: (t.astype(BF16), (t - t.astype(BF16).astype(F32)).astype(BF16))
    wup_hi, wup_lo = split(wup)
    aup_hi, aup_lo = split(aup)
    keyed_shape = jax.ShapeDtypeStruct((N // XF_T, RWKV_HEAD, XF_T, LANES), F32)
    keyed_spec = pl.BlockSpec((None, RWKV_HEAD // KEY_GROUPS, XF_T, LANES), lambda i, g: (i, g, 0, 0))
    return pl.pallas_call(
        _rwkv_prep_kernel,
        grid=(N // XF_T, KEY_GROUPS),
        in_specs=[
            tok_spec(A_RK_R), tok_spec(A_RK_K), tok_spec(A_RK_V),
            pl.BlockSpec((B, XF_T, LORA_PAD), lambda i, g: (0, i, 0)),
            const((2, LORA_PAD, BRANCH_W)), const((2, LORA_PAD, BRANCH_W)),
            const((2, LORA_PAD, BRANCH_W)), const((2, LORA_PAD, BRANCH_W)),
            const((2, BRANCH_W)), const((2, BRANCH_W)),
            const((1, BRANCH_W)), const((1, BRANCH_W)), const((1, BRANCH_W)),
            const((BRANCH_W, BRANCH_W)),
        ],
        out_specs=[keyed_spec] * len(KEYED) + [
            pl.BlockSpec((XF_T, SCAN_Q, LANES), lambda i, g: (i, 0, 0)),
            pl.BlockSpec((B, XF_T, BRANCH_W), lambda i, g: (0, i, 0))],
        out_shape=[keyed_shape] * len(KEYED) + [
            jax.ShapeDtypeStruct((N, SCAN_Q, LANES), F32),
            jax.ShapeDtypeStruct((B, N, BRANCH_W), F32)],
        scratch_shapes=[pltpu.VMEM(((len(KEYED) + 1) * B, BRANCH_W, XF_T), F32)],
        compiler_params=_cparams(("arbitrary", "arbitrary")),
        name="rwkv_prep",
    )(pa, pa, pa, lora, wup_hi, wup_lo, aup_hi, aup_lo, w0, a0,
      k_k.reshape(1, -1), k_a.reshape(1, -1), r_k.reshape(1, -1), ones)


def _rwkv_scan_kernel(wf, kkaf, kdf, wb, kkab, kdb, rf, kkf, rb, kkb, kkn_f, kkn_b, vf, vb,
                      of_ref, ob_ref, ckf, ckb, sf_ref, sb_ref, saf_ref, sab_ref):
    @pl.when(pl.program_id(0) == 0)
    def _():
        for ref in (sf_ref, sb_ref, saf_ref, sab_ref):
            ref[...] = jnp.zeros_like(ref)

    ckf[:, 0:SCAN_T, :] = kkf[...]
    ckf[:, SCAN_T:, :] = kkn_f[...]
    ckb[:, 0:SUBLANES, :] = kkn_b[...]
    ckb[:, SUBLANES:, :] = kkb[...]

    def chain(s_ref, sa, v_t, w, kka, kd, r, kk_next, t, t_next):
        o = jnp.zeros((SCAN_Q, LANES), F32)
        nsa = jnp.zeros((SCAN_Q, LANES), F32)
        row = pl.ds(t, 1)
        row_next = pl.ds(t_next, 1)
        for k in range(RWKV_HEAD):
            s = s_ref[k] * w[k, row, :] + sa * kka[k, row, :] + v_t * kd[k, row, :]
            s_ref[k] = s
            o = o + s * r[k, row, :]
            nsa = nsa + s * kk_next[k, row_next, :]
        return o, -nsa

    def step(t, carry):
        sa_f, sa_b = carry
        tb = SCAN_T - 1 - t
        o_f, sa_f = chain(sf_ref, sa_f, vf[t], wf, kkaf, kdf, rf, ckf, t, t + 1)
        o_b, sa_b = chain(sb_ref, sa_b, vb[tb], wb, kkab, kdb, rb, ckb, tb, tb + SUBLANES - 1)
        of_ref[t] = o_f
        ob_ref[tb] = o_b
        return sa_f, sa_b

    saf_ref[...], sab_ref[...] = lax.fori_loop(0, SCAN_T, step, (saf_ref[...], sab_ref[...]))


def _rwkv_scan(z, zv, n_ctx):
    N = zv.shape[0]
    nb, nc = N // SCAN_T, n_ctx // SCAN_T
    per_tile = XF_T // SCAN_T

    def mirror(i):
        return jnp.where(i < nc, nc - 1 - i, nb + nc - 1 - i)

    def tok_f(i):
        return jnp.minimum((i + 1) * SCAN_T, N - SUBLANES)

    def tok_b(i):
        return mirror(jnp.minimum(i + 1, nb - 1)) * SCAN_T + SCAN_T - 1

    def keyed(block_of):
        return pl.BlockSpec((None, RWKV_HEAD, SCAN_T, LANES),
                            lambda i: (block_of(i) // per_tile, 0, block_of(i) % per_tile, 0))

    def group(tok_of):
        return pl.BlockSpec((None, RWKV_HEAD, SUBLANES, LANES),
                            lambda i: (tok_of(i) // XF_T, 0, (tok_of(i) % XF_T) // SUBLANES, 0))

    fwd, bwd = keyed(lambda i: i), keyed(mirror)
    vf = pl.BlockSpec((SCAN_T, SCAN_Q, LANES), lambda i: (i, 0, 0))
    vb = pl.BlockSpec((SCAN_T, SCAN_Q, LANES), lambda i: (mirror(i), 0, 0))
    out = jax.ShapeDtypeStruct((N, SCAN_Q, LANES), F32)
    state = pltpu.VMEM((RWKV_HEAD, SCAN_Q, LANES), F32)
    return pl.pallas_call(
        _rwkv_scan_kernel,
        grid=(nb,),
        in_specs=[fwd] * 3 + [bwd] * 3 + [fwd, fwd, bwd, bwd, group(tok_f), group(tok_b), vf, vb],
        out_specs=[vf, vb],
        out_shape=[out, out],
        scratch_shapes=[pltpu.VMEM((RWKV_HEAD, SCAN_T + SUBLANES, LANES), F32)] * 2 + [state, state] + [
            pltpu.VMEM((SCAN_Q, LANES), F32)] * 2,
        compiler_params=_cparams(("arbitrary",)),
        name="rwkv_scan",
    )(z["w0"], z["kka0"], z["kd0"], z["w1"], z["kka1"], z["kd1"], z["r"], z["kk"], z["r"], z["kk"],
      z["kk"], z["kk"], zv, zv)


def _rwkv_post_kernel(of_ref, ob_ref, bonus_ref, gate_ref, g_ref, b_ref, ones_ref, o_ref, y_ref):
    n_b = bonus_ref.shape[0]
    copies = LANES // (n_b * RWKV_HEADS)
    for val in range(SCAN_Q):
        t = (of_ref[:, val, :] + ob_ref[:, val, :]).T
        for q in range(copies):
            for b in range(n_b):
                lo = (q * n_b + b) * RWKV_HEADS
                y_ref[b, pl.ds(q * SCAN_Q + val, RWKV_HEADS, stride=RWKV_HEAD), :] = t[lo:lo + RWKV_HEADS]
    ones = ones_ref[...]
    for b in range(n_b):
        y = y_ref[b].T
        mu = _split_dot(y, ones) * (1.0 / RWKV_HEAD)
        d = y - mu
        var = _split_dot(d * d, ones) * (1.0 / RWKV_HEAD)
        yn = d * lax.rsqrt(var + RWKV_GN_EPS) * g_ref[...] + b_ref[...]
        o_ref[b] = ((yn + bonus_ref[b]) * _silu(gate_ref[b].astype(F32))).astype(o_ref.dtype)


def _rwkv_post(o_f, o_b, bonus, pb, gn_g, gn_b, ones):
    B, N, _ = bonus.shape
    nat = pl.BlockSpec((B, XF_T, BRANCH_W), lambda i: (0, i, 0))
    scan = pl.BlockSpec((XF_T, SCAN_Q, LANES), lambda i: (i, 0, 0))
    const = lambda shape: pl.BlockSpec(shape, lambda i: (0, 0))
    return pl.pallas_call(
        _rwkv_post_kernel,
        grid=(N // XF_T,),
        in_specs=[scan, scan, nat,
                  pl.BlockSpec((B, XF_T, BRANCH_W), lambda i: (0, i, B_RK_G // BRANCH_W)),
                  const((1, BRANCH_W)), const((1, BRANCH_W)), const((BRANCH_W, BRANCH_W))],
        out_specs=nat,
        out_shape=jax.ShapeDtypeStruct((B, N, BRANCH_W), BF16),
        scratch_shapes=[pltpu.VMEM((B, BRANCH_W, XF_T), F32)],
        compiler_params=_cparams(("arbitrary",)),
        name="rwkv_post",
    )(o_f, o_b, bonus, pb, gn_g.reshape(1, -1), gn_b.reshape(1, -1), ones)


def _rwkv_branch(pa, pb, lora, li, p, ones, n_ctx):
    B, N, _ = pa.shape
    assert LANES % (B * RWKV_HEADS) == 0 and LANES // (B * RWKV_HEADS) * SCAN_Q == RWKV_HEAD
    assert n_ctx % SCAN_T == 0 and N % XF_T == 0 and XF_T % SCAN_T == 0
    outs = _rwkv_prep(pa, lora, p["wup"][li], p["aup"][li], p["rwkv_w0"][li], p["rwkv_a0"][li],
                      p["rwkv_k_k"][li], p["rwkv_k_a"][li], p["rwkv_r_k"][li], ones)
    z = dict(zip(KEYED, outs[:len(KEYED)]))
    zv, bonus = outs[len(KEYED):]
    o_f, o_b = _rwkv_scan(z, zv, n_ctx)
    return _rwkv_post(o_f, o_b, bonus, pb, p["rwkv_gn_g"][li], p["rwkv_gn_b"][li], ones)


def _merge_kernel(bc_ref, br_ref, bd_ref, bq_ref, lg_ref, wb_ref, bg_ref, wo_ref, x_ref, gp_ref, gt_ref,
                  sh_ref, sc_ref, gn_ref, o_ref, h_ref):
    merged = jnp.zeros((TOK, D_MODEL), F32)
    for j, b_ref in enumerate((bc_ref, br_ref, bd_ref, bq_ref)):
        pj = jnp.dot(b_ref[...], wb_ref[j], preferred_element_type=F32)
        gj = jax.nn.sigmoid(lg_ref[:, j * D_MODEL:(j + 1) * D_MODEL].astype(F32) + bg_ref[j:j + 1, :])
        merged = merged + pj * gj
    y = jnp.dot(merged.astype(BF16), wo_ref[...], preferred_element_type=F32)
    y = y * lax.rsqrt(jnp.mean(y * y, axis=-1, keepdims=True) + NORM_EPS) * gp_ref[...]
    x = x_ref[...] + gt_ref[...] * y
    o_ref[...] = x
    hn = x * lax.rsqrt(jnp.mean(x * x, axis=-1, keepdims=True) + NORM_EPS) * gn_ref[...]
    h_ref[...] = (hn * (1.0 + sc_ref[...]) + sh_ref[...]).astype(h_ref.dtype)


def _merge(branches, pc, w_branch, b_gate, w_out, li, xa, g_post, mod_l, mod_next, g_next):
    B, N, _ = xa.shape
    row = _mod_row(B)
    bspec = pl.BlockSpec((None, TOK, BRANCH_W), lambda b, i: (b, i, 0))
    xspec = pl.BlockSpec((None, TOK, D_MODEL), lambda b, i: (b, i, 0))
    mod_spec = lambda part: pl.BlockSpec((None, 1, D_MODEL), lambda b, i: row(b, i)[:2] + (part,))
    vec_spec = pl.BlockSpec((1, D_MODEL), lambda b, i: (0, 0))
    once = pl.Buffered(1)
    return pl.pallas_call(
        _merge_kernel,
        grid=(B, N // TOK),
        in_specs=[bspec] * 4 + [
            pl.BlockSpec((None, TOK, N_BRANCH * D_MODEL), lambda b, i: (b, i, 0)),
            pl.BlockSpec((None, N_BRANCH, BRANCH_W, D_MODEL), lambda b, i: (li, 0, 0, 0), pipeline_mode=once),
            pl.BlockSpec((None, N_BRANCH, D_MODEL), lambda b, i: (li, 0, 0)),
            pl.BlockSpec((None, D_MODEL, D_MODEL), lambda b, i: (li, 0, 0), pipeline_mode=once),
            xspec, vec_spec, mod_spec(2), mod_spec(0), mod_spec(1), vec_spec,
        ],
        out_specs=[xspec, xspec],
        out_shape=[jax.ShapeDtypeStruct((B, N, D_MODEL), F32), jax.ShapeDtypeStruct((B, N, D_MODEL), BF16)],
        compiler_params=_cparams(("arbitrary", "arbitrary")),
        name="merge",
    )(*branches, pc, w_branch, b_gate, w_out, xa, g_post.reshape(1, -1), mod_l, mod_next, mod_next,
      g_next.reshape(1, -1))


def kernel(x, c, ctx, c_ctx, norm_pre_g, norm_post_g, w_mod, b_mod, w_in, conv_w, conv_b, conv_ln_g, conv_ln_b,
           rwkv_w0, rwkv_w_up, rwkv_a0, rwkv_a_up, rwkv_k_k, rwkv_k_a, rwkv_r_k, rwkv_gn_g, rwkv_gn_b,
           diff_lam, diff_subln_g, gqa_qk_norm_g, w_branch, b_gate, w_out):
    B, n_lat, _ = x.shape
    n_ctx = ctx.shape[1]
    depth = w_in.shape[0]
    N = n_ctx + n_lat
    assert n_ctx == TOK and n_lat % TOK == 0 and B + 1 <= MOD_ROWS and w_in.shape[-1] == D_IN

    w_in_b = w_in.astype(BF16)
    w_branch_b = w_branch.astype(BF16)
    w_out_b = w_out.astype(BF16)
    zrows = lambda t, lo: jnp.pad(t, ((0, 0), (0, 0), (lo, LORA_PAD - lo - t.shape[2]), (0, 0)))
    params = dict(wup=zrows(rwkv_w_up, 0), aup=zrows(rwkv_a_up, DECAY_LORA), rwkv_w0=rwkv_w0, rwkv_a0=rwkv_a0,
                  rwkv_k_k=rwkv_k_k, rwkv_k_a=rwkv_k_a, rwkv_r_k=rwkv_r_k.reshape(depth, BRANCH_W),
                  rwkv_gn_g=rwkv_gn_g, rwkv_gn_b=rwkv_gn_b)
    head = np.arange(BRANCH_W) // RWKV_HEAD
    ones = jnp.asarray(head[:, None] == head[None, :], BF16)
    cos_d, sin_d = _rope_tables(n_ctx, n_lat, DIFF_HALF, 2)
    cos_g, sin_g = _rope_tables(n_ctx, n_lat, GQA_HEAD, 1)

    cvec = jnp.concatenate([c, c_ctx[None, :], jnp.zeros((MOD_ROWS - B - 1, D_MODEL), c.dtype)], axis=0)
    mod = _modulation(cvec, w_mod, b_mod)
    xa = jnp.concatenate([ctx, x], axis=1)
    tm = 2176 if (B * N) % 2176 == 0 else TOK

    h = _prenorm(xa, mod[0], norm_pre_g[0])
    for li in range(depth):
        h = h.reshape(B * N, D_MODEL)
        pa = _project(h, w_in_b, li, O_A, O_LORA - O_A, 1024, BF16, tm, (B, N, O_LORA - O_A))
        pb = _project(h, w_in_b, li, O_B, O_C - O_B, 768, BF16, tm, (B, N, O_C - O_B))
        pc = _project(h, w_in_b, li, O_C, D_IN - O_C, 512, BF16, tm, (B, N, D_IN - O_C))
        lora = _project(h, w_in_b, li, O_LORA, LORA_PAD, LORA_PAD, F32, tm, (B, N, LORA_PAD))
        lam_init = 0.8 - 0.6 * math.exp(-0.3 * li)
        br_conv = _conv_branch(pa, conv_w[li], conv_b[li], conv_ln_g[li], conv_ln_b[li])
        br_rwkv = _rwkv_branch(pa, pb, lora, li, params, ones, n_ctx)
        br_diff = _diff_branch(pb, diff_lam[li], diff_subln_g[li], cos_d, sin_d, n_ctx, lam_init)
        br_gqa = _gqa_branch(pb, gqa_qk_norm_g[li], cos_g, sin_g, n_ctx)
        nxt = min(li + 1, depth - 1)
        xa, h = _merge((br_conv, br_rwkv, br_diff, br_gqa), pc, w_branch_b, b_gate, w_out_b, li, xa,
                       norm_post_g[li], mod[li], mod[nxt], norm_pre_g[nxt])
    return xa[:, n_ctx:]
```

```python
import functools
import math

import jax
import jax.numpy as jnp
import numpy as np
from jax import lax
from jax.experimental import pallas as pl
from jax.experimental.pallas import tpu as pltpu

F32 = jnp.float32
BF16 = jnp.bfloat16

D_MODEL = 2048
BRANCH_W = D_MODEL // 2
N_BRANCH = 4
GRID_W = 64
CONV_K = 31
CONV_HALO = 16
RWKV_HEAD = 64
RWKV_HEADS = BRANCH_W // RWKV_HEAD
DECAY_LORA = 96
AAA_LORA = 96
LORA_PAD = 256
RWKV_GN_EPS = 64e-5
DIFF_HALF = 64
DIFF_VDIM = 2 * DIFF_HALF
DIFF_HEADS = BRANCH_W // DIFF_VDIM
GQA_HEAD = 128
GQA_HEADS = BRANCH_W // GQA_HEAD
GQA_KV_HEADS = GQA_HEADS // 4
GQA_REP = GQA_HEADS // GQA_KV_HEADS
ROPE_THETA = 10000.0
NORM_EPS = 1e-6

TOK = 256
LANES = 128
SUBLANES = 8
MOD_ROWS = 8
VMEM_LIMIT = 60 * 1024 * 1024

O_A = 0
O_LORA = 6 * BRANCH_W
O_B = O_LORA + DECAY_LORA + AAA_LORA
O_C = O_B + 6 * BRANCH_W + 2 * GQA_KV_HEADS * GQA_HEAD + BRANCH_W
D_IN = O_C + N_BRANCH * D_MODEL
A_CONV, A_RK_R, A_RK_K, A_RK_V = 0, 3 * BRANCH_W, 4 * BRANCH_W, 5 * BRANCH_W
B_RK_G, B_DF_Q, B_DF_K, B_DF_V, B_DF_G, B_GQ_Q = (j * BRANCH_W for j in range(6))
B_GQ_K = 6 * BRANCH_W
B_GQ_V = B_GQ_K + GQA_KV_HEADS * GQA_HEAD
B_GQ_G = B_GQ_V + GQA_KV_HEADS * GQA_HEAD


def _cparams(sem):
    return pltpu.CompilerParams(dimension_semantics=sem, vmem_limit_bytes=VMEM_LIMIT)


def _silu(t):
    return t * jax.nn.sigmoid(t)


def _split_dot(a, b_hi, b_lo=None):
    a_hi = a.astype(BF16)
    a_lo = (a - a_hi.astype(F32)).astype(BF16)
    out = jnp.dot(a_hi, b_hi, preferred_element_type=F32) + jnp.dot(a_lo, b_hi, preferred_element_type=F32)
    if b_lo is not None:
        out = out + jnp.dot(a_hi, b_lo, preferred_element_type=F32)
    return out


def _mod_kernel(c_ref, w_ref, b_ref, o_ref):
    cv = c_ref[...]
    s = _silu(cv).astype(BF16)
    o_ref[...] = jnp.dot(s, w_ref[...].astype(BF16), preferred_element_type=F32) + b_ref[...]


def _modulation(cvec, w_mod, b_mod):
    L = w_mod.shape[0]
    tn = 1536
    out = pl.pallas_call(
        _mod_kernel,
        grid=(L, 3 * D_MODEL // tn),
        in_specs=[
            pl.BlockSpec((MOD_ROWS, D_MODEL), lambda l, j: (0, 0)),
            pl.BlockSpec((None, D_MODEL, tn), lambda l, j: (l, 0, j)),
            pl.BlockSpec((None, 1, tn), lambda l, j: (l, 0, j)),
        ],
        out_specs=pl.BlockSpec((None, MOD_ROWS, tn), lambda l, j: (l, 0, j)),
        out_shape=jax.ShapeDtypeStruct((L, MOD_ROWS, 3 * D_MODEL), F32),
        compiler_params=_cparams(("arbitrary", "arbitrary")),
        name="modulation",
    )(cvec, w_mod, b_mod.reshape(L, 1, 3 * D_MODEL))
    return out.reshape(L, MOD_ROWS, 1, 3 * D_MODEL)


def _mod_row(batch):
    return lambda b, i: (jnp.where(i == 0, batch, b), 0, 0)


def _prenorm_kernel(x_ref, sh_ref, sc_ref, g_ref, h_ref):
    x = x_ref[...]
    y = x * lax.rsqrt(jnp.mean(x * x, axis=-1, keepdims=True) + NORM_EPS) * g_ref[...]
    h_ref[...] = (y * (1.0 + sc_ref[...]) + sh_ref[...]).astype(h_ref.dtype)


def _prenorm(xa, mod_l, g):
    B, N, _ = xa.shape
    row = _mod_row(B)
    return pl.pallas_call(
        _prenorm_kernel,
        grid=(B, N // TOK),
        in_specs=[
            pl.BlockSpec((None, TOK, D_MODEL), lambda b, i: (b, i, 0)),
            pl.BlockSpec((None, 1, D_MODEL), lambda b, i: row(b, i)[:2] + (0,)),
            pl.BlockSpec((None, 1, D_MODEL), lambda b, i: row(b, i)[:2] + (1,)),
            pl.BlockSpec((1, D_MODEL), lambda b, i: (0, 0)),
        ],
        out_specs=pl.BlockSpec((None, TOK, D_MODEL), lambda b, i: (b, i, 0)),
        out_shape=jax.ShapeDtypeStruct((B, N, D_MODEL), BF16),
        compiler_params=_cparams(("arbitrary", "arbitrary")),
        name="prenorm",
    )(xa, mod_l, mod_l, g.reshape(1, D_MODEL))


def _mm_kernel(a_ref, w_ref, o_ref):
    o_ref[...] = jnp.dot(a_ref[...], w_ref[...].astype(BF16), preferred_element_type=F32).astype(o_ref.dtype)


def _mm_shift_kernel(a_ref, w_lo_ref, w_hi_ref, o_ref, w_ref, *, shift):
    @pl.when(pl.program_id(1) == 0)
    def _():
        w_ref[...] = jnp.concatenate([w_lo_ref[:, shift:], w_hi_ref[:, :shift]], axis=1).astype(BF16)

    o_ref[...] = jnp.dot(a_ref[...], w_ref[...], preferred_element_type=F32).astype(o_ref.dtype)


def _project(h, w_all, li, col0, ncols, tn, out_dtype, tm, shape3):
    M, K = h.shape
    first, shift = divmod(col0, tn)
    w_spec = lambda off: pl.BlockSpec((None, K, tn), lambda j, i: (li, 0, first + off + j))
    if shift == 0:
        body, w_specs, scratch = _mm_kernel, [w_spec(0)], []
    else:
        body = functools.partial(_mm_shift_kernel, shift=shift)
        w_specs, scratch = [w_spec(0), w_spec(1)], [pltpu.VMEM((K, tn), BF16)]
    out = pl.pallas_call(
        body,
        grid=(ncols // tn, M // tm),
        in_specs=[pl.BlockSpec((tm, K), lambda j, i: (i, 0))] + w_specs,
        out_specs=pl.BlockSpec((tm, tn), lambda j, i: (i, j)),
        out_shape=jax.ShapeDtypeStruct((M, ncols), out_dtype),
        scratch_shapes=scratch,
        compiler_params=_cparams(("arbitrary", "arbitrary")),
        name="in_proj",
    )(h, *([w_all] * len(w_specs)))
    return out.reshape(shape3)


def _conv_kernel(cur_ref, gate_ref, prev_ref, next_ref, cw_ref, cb_ref, lg_ref, lb_ref, o_ref, buf_ref, *, n_tiles):
    i = pl.program_id(1)

    def glu(t):
        return t[:, :BRANCH_W].astype(F32) * jax.nn.sigmoid(t[:, BRANCH_W:2 * BRANCH_W].astype(F32))

    has_prev = jnp.logical_and(i != 0, i != 1)
    has_next = jnp.logical_and(i != 0, i != n_tiles - 1)
    cur = cur_ref[...]
    buf_ref[0:CONV_HALO, :] = jnp.where(has_prev, glu(prev_ref[...]), 0.0)
    buf_ref[CONV_HALO:CONV_HALO + TOK, :] = glu(cur)
    buf_ref[CONV_HALO + TOK:, :] = jnp.where(has_next, glu(next_ref[...]), 0.0)
    base = CONV_HALO - CONV_K // 2
    acc = jnp.broadcast_to(cb_ref[...], (TOK, BRANCH_W))
    for r in range(SUBLANES):
        part = None
        for j in range(CONV_K):
            if (base + j) % SUBLANES == r:
                lo = base + j - r
                term = cw_ref[j:j + 1, :] * buf_ref[lo:lo + TOK + SUBLANES, :]
                part = term if part is None else part + term
        if part is not None:
            acc = acc + part[r:r + TOK]
    mu = jnp.mean(acc, axis=-1, keepdims=True)
    d = acc - mu
    var = jnp.mean(d * d, axis=-1, keepdims=True)
    y = d * lax.rsqrt(var + 1e-5) * lg_ref[...] + lb_ref[...]
    o_ref[...] = (_silu(y) * _silu(gate_ref[...].astype(F32))).astype(o_ref.dtype)


def _conv_branch(pa, conv_w, conv_b, ln_g, ln_b):
    B, N, _ = pa.shape
    nt = N // TOK
    hpt = TOK // CONV_HALO
    last = N // CONV_HALO - 1
    cblk = A_CONV // (2 * BRANCH_W)
    return pl.pallas_call(
        functools.partial(_conv_kernel, n_tiles=nt),
        grid=(B, nt),
        in_specs=[
            pl.BlockSpec((None, TOK, 2 * BRANCH_W), lambda b, i: (b, i, cblk)),
            pl.BlockSpec((None, TOK, BRANCH_W), lambda b, i: (b, i, 2 * cblk + 2)),
            pl.BlockSpec((None, CONV_HALO, 2 * BRANCH_W), lambda b, i: (b, jnp.maximum(i * hpt - 1, 0), cblk)),
            pl.BlockSpec((None, CONV_HALO, 2 * BRANCH_W), lambda b, i: (b, jnp.minimum((i + 1) * hpt, last), cblk)),
            pl.BlockSpec((CONV_K, BRANCH_W), lambda b, i: (0, 0)),
            pl.BlockSpec((1, BRANCH_W), lambda b, i: (0, 0)),
            pl.BlockSpec((1, BRANCH_W), lambda b, i: (0, 0)),
            pl.BlockSpec((1, BRANCH_W), lambda b, i: (0, 0)),
        ],
        out_specs=pl.BlockSpec((None, TOK, BRANCH_W), lambda b, i: (b, i, 0)),
        out_shape=jax.ShapeDtypeStruct((B, N, BRANCH_W), BF16),
        scratch_shapes=[pltpu.VMEM((TOK + 2 * CONV_HALO, BRANCH_W), F32)],
        compiler_params=_cparams(("arbitrary", "arbitrary")),
        name="conv_branch",
    )(pa, pa, pa, pa, conv_w, conv_b.reshape(1, -1), ln_g.reshape(1, -1), ln_b.reshape(1, -1))


def _rope_tables(n_ctx, n_lat, dim, reps):
    half = dim // 2
    t = np.arange(n_lat)
    inv = ROPE_THETA ** (-np.arange(0, half, 2, dtype=np.float64) / half)
    ang = np.concatenate([(t // GRID_W)[:, None] * inv, (t % GRID_W)[:, None] * inv], axis=-1)
    ang = np.concatenate([np.zeros((n_ctx, half)), ang], axis=0)
    cos = np.repeat(np.cos(ang), 2, axis=-1)
    sin = np.repeat(np.sin(ang), 2, axis=-1) * np.tile(np.array([-1.0, 1.0]), half)
    return (jnp.asarray(np.tile(cos, (1, reps)), F32), jnp.asarray(np.tile(sin, (1, reps)), F32))


def _rope(t, cos, sin_signed):
    lane = lax.broadcasted_iota(jnp.int32, t.shape, 1)
    partner = jnp.where(lane % 2 == 0, pltpu.roll(t, LANES - 1, 1), pltpu.roll(t, 1, 1))
    return t * cos + partner * sin_signed


LOG2_E = 1.4426950408889634


ATTN_CHUNK = 1536


def _scores_t(k, q):
    return lax.dot_general(k, q, (((1,), (1,)), ((), ())), preferred_element_type=F32)


def _online_attention(kp_ref, vt_ref, qs, nk):
    chunks = [(lo, min(ATTN_CHUNK, nk - lo)) for lo in range(0, nk, ATTN_CHUNK)]

    def scores(c):
        lo, n = chunks[c]
        k = kp_ref[lo:lo + n, :]
        return [_scores_t(k, q) for q in qs]

    state = [None] * len(qs)
    s_next = scores(0)
    for c, (lo, n) in enumerate(chunks):
        s_cur = s_next
        if c + 1 < len(chunks):
            s_next = scores(c + 1)
        v = vt_ref[:, lo:lo + n]
        for j, s in enumerate(s_cur):
            m = jnp.max(s, axis=0, keepdims=True)
            if state[j] is not None:
                m_old, l_old, acc_old = state[j]
                m = jnp.maximum(m_old, m)
            e = jnp.exp2(s - m)
            l = jnp.sum(e, axis=0, keepdims=True)
            acc = jnp.dot(v, e.astype(BF16), preferred_element_type=F32)
            if state[j] is not None:
                alpha = jnp.exp2(m_old - m)
                l = l_old * alpha + l
                acc = acc_old * alpha + acc
            state[j] = (m, l, acc)
    return [(acc, l) for _, l, acc in state]


def _gqa_kernel(q_ref, k_ref, v_ref, gate_ref, cos_ref, sin_ref, g_ref, o_ref, kp_ref, vt_ref, *, n_ctx, n_all):
    r = pl.program_id(2)
    qi = pl.program_id(3)

    @pl.when(jnp.logical_and(r == 0, qi == 0))
    def _():
        k = k_ref[...].astype(F32)
        k = k * lax.rsqrt(jnp.mean(k * k, axis=-1, keepdims=True) + NORM_EPS) * g_ref[1:2, :]
        kp_ref[...] = _rope(k, cos_ref[...], sin_ref[...]).astype(BF16)
        vt_ref[...] = v_ref[...].astype(F32).T.astype(BF16)

    rows = pl.ds(pl.multiple_of(qi * TOK, TOK), TOK)
    qs = []
    for j in range(GQA_PAIR):
        q = q_ref[:, j * GQA_HEAD:(j + 1) * GQA_HEAD].astype(F32)
        q = q * lax.rsqrt(jnp.mean(q * q, axis=-1, keepdims=True) + NORM_EPS) * g_ref[0:1, :]
        q = _rope(q, cos_ref[rows, :], sin_ref[rows, :]) * (GQA_HEAD ** -0.5 * LOG2_E)
        qs.append(q.astype(BF16))
    gate = _silu(gate_ref[...].astype(F32))

    def attend(nk):
        o = jnp.concatenate([(acc / l).T for acc, l in _online_attention(kp_ref, vt_ref, qs, nk)], axis=1)
        o_ref[...] = (o * gate).astype(o_ref.dtype)

    pl.when(qi == 0)(lambda: attend(n_ctx))
    pl.when(qi != 0)(lambda: attend(n_all))


GQA_PAIR = 2


def _gqa_branch(pb, qk_g, cos, sin, n_ctx):
    B, N, _ = pb.shape
    blk = lambda c0: c0 // GQA_HEAD
    wide = GQA_PAIR * GQA_HEAD
    steps = GQA_REP // GQA_PAIR
    return pl.pallas_call(
        functools.partial(_gqa_kernel, n_ctx=n_ctx, n_all=N),
        grid=(B, GQA_KV_HEADS, steps, N // TOK),
        in_specs=[
            pl.BlockSpec((None, TOK, wide), lambda b, g, r, i: (b, i, B_GQ_Q // wide + g * steps + r)),
            pl.BlockSpec((None, N, GQA_HEAD), lambda b, g, r, i: (b, 0, blk(B_GQ_K) + g)),
            pl.BlockSpec((None, N, GQA_HEAD), lambda b, g, r, i: (b, 0, blk(B_GQ_V) + g)),
            pl.BlockSpec((None, TOK, wide), lambda b, g, r, i: (b, i, B_GQ_G // wide + g * steps + r)),
            pl.BlockSpec((N, GQA_HEAD), lambda b, g, r, i: (0, 0)),
            pl.BlockSpec((N, GQA_HEAD), lambda b, g, r, i: (0, 0)),
            pl.BlockSpec((2, GQA_HEAD), lambda b, g, r, i: (0, 0)),
        ],
        out_specs=pl.BlockSpec((None, TOK, wide), lambda b, g, r, i: (b, i, g * steps + r)),
        out_shape=jax.ShapeDtypeStruct((B, N, BRANCH_W), BF16),
        scratch_shapes=[pltpu.VMEM((N, GQA_HEAD), BF16), pltpu.VMEM((GQA_HEAD, N), BF16)],
        compiler_params=_cparams(("arbitrary",) * 4),
        name="gqa_branch",
    )(pb, pb, pb, pb, cos, sin, qk_g)


def _diff_kernel(q_ref, k_ref, v_ref, gate_ref, cos_ref, sin_ref, lam_ref, sg_ref, o_ref, kp_ref, vt_ref,
                 *, n_ctx, n_all, lam_init):
    qi = pl.program_id(2)

    @pl.when(qi == 0)
    def _():
        kp_ref[...] = _rope(k_ref[...].astype(F32), cos_ref[...], sin_ref[...]).astype(BF16)
        vt_ref[...] = v_ref[...].astype(F32).T.astype(BF16)

    lp = lam_ref[...]
    lam = (jnp.exp(jnp.sum(lp[0:1] * lp[1:2], axis=-1, keepdims=True))
           - jnp.exp(jnp.sum(lp[2:3] * lp[3:4], axis=-1, keepdims=True)) + lam_init)
    rows = pl.ds(pl.multiple_of(qi * TOK, TOK), TOK)
    q = _rope(q_ref[...].astype(F32), cos_ref[rows, :], sin_ref[rows, :]) * (DIFF_HALF ** -0.5 * LOG2_E)
    lane = lax.broadcasted_iota(jnp.int32, q.shape, 1)
    q1 = jnp.where(lane < DIFF_HALF, q, 0.0).astype(BF16)
    q2 = jnp.where(lane >= DIFF_HALF, q, 0.0).astype(BF16)
    gate = _silu(gate_ref[...].astype(F32))

    def attend(nk):
        (acc1, l1), (acc2, l2) = _online_attention(kp_ref, vt_ref, (q1, q2), nk)
        o = (acc1 / l1 - acc2 * (lam / l2)).T
        o = o * lax.rsqrt(jnp.mean(o * o, axis=-1, keepdims=True) + NORM_EPS) * sg_ref[...]
        o_ref[...] = (o * (1.0 - lam_init) * gate).astype(o_ref.dtype)

    pl.when(qi == 0)(lambda: attend(n_ctx))
    pl.when(qi != 0)(lambda: attend(n_all))


def _diff_branch(pb, lam_p, subln_g, cos, sin, n_ctx, lam_init):
    B, N, _ = pb.shape
    blk = lambda c0: c0 // DIFF_VDIM
    return pl.pallas_call(
        functools.partial(_diff_kernel, n_ctx=n_ctx, n_all=N, lam_init=lam_init),
        grid=(B, DIFF_HEADS, N // TOK),
        in_specs=[
            pl.BlockSpec((None, TOK, DIFF_VDIM), lambda b, h, i: (b, i, blk(B_DF_Q) + h)),
            pl.BlockSpec((None, N, DIFF_VDIM), lambda b, h, i: (b, 0, blk(B_DF_K) + h)),
            pl.BlockSpec((None, N, DIFF_VDIM), lambda b, h, i: (b, 0, blk(B_DF_V) + h)),
            pl.BlockSpec((None, TOK, DIFF_VDIM), lambda b, h, i: (b, i, blk(B_DF_G) + h)),
            pl.BlockSpec((N, DIFF_VDIM), lambda b, h, i: (0, 0)),
            pl.BlockSpec((N, DIFF_VDIM), lambda b, h, i: (0, 0)),
            pl.BlockSpec((4, DIFF_HALF), lambda b, h, i: (0, 0)),
            pl.BlockSpec((1, DIFF_VDIM), lambda b, h, i: (0, 0)),
        ],
        out_specs=pl.BlockSpec((None, TOK, DIFF_VDIM), lambda b, h, i: (b, i, h)),
        out_shape=jax.ShapeDtypeStruct((B, N, BRANCH_W), BF16),
        scratch_shapes=[pltpu.VMEM((N, DIFF_VDIM), BF16), pltpu.VMEM((DIFF_VDIM, N), BF16)],
        compiler_params=_cparams(("arbitrary",) * 3),
        name="diff_branch",
    )(pb, pb, pb, pb, cos, sin, lam_p, subln_g.reshape(1, DIFF_VDIM))


XF_T = 128
SCAN_T = 64
SCAN_Q = RWKV_HEAD // 4
KEYED = ("w0", "kka0", "kd0", "w1", "kka1", "kd1", "r", "kk")
KEY_GROUPS = 4


def _rwkv_prep_kernel(r_ref, k_ref, v_ref, lora_ref, wup_hi_ref, wup_lo_ref, aup_hi_ref, aup_lo_ref, w0_ref, a0_ref,
                      kk_g_ref, ka_ref, rk_ref, ones_ref, *refs):
    out = dict(zip(KEYED, refs[:len(KEYED)]))
    zv_ref, bonus_ref, y_ref = refs[len(KEYED):]
    n_b = r_ref.shape[0]
    slot = {name: j * n_b for j, name in enumerate(KEYED + ("v",))}
    copies = LANES // (n_b * RWKV_HEADS)
    group = pl.program_id(1)

    @pl.when(group == 0)
    def _():
        _rwkv_prep_tile(r_ref, k_ref, v_ref, lora_ref, wup_hi_ref, wup_lo_ref, aup_hi_ref, aup_lo_ref, w0_ref, a0_ref,
                        kk_g_ref, ka_ref, rk_ref, ones_ref, bonus_ref, y_ref, slot)
        for val in range(SCAN_Q):
            rows = [y_ref[slot["v"] + b, pl.ds(q * SCAN_Q + val, RWKV_HEADS, stride=RWKV_HEAD), :]
                    for q in range(copies) for b in range(n_b)]
            zv_ref[:, val, :] = jnp.concatenate(rows, axis=0).T

    for g in range(KEY_GROUPS):
        @pl.when(group == g)
        def _(g=g):
            for name in KEYED:
                for j in range(RWKV_HEAD // KEY_GROUPS):
                    key = g * (RWKV_HEAD // KEY_GROUPS) + j
                    rows = [y_ref[slot[name] + b, pl.ds(key, RWKV_HEADS, stride=RWKV_HEAD), :] for b in range(n_b)]
                    out[name][j] = jnp.concatenate(rows * copies, axis=0).T


def _rwkv_prep_tile(r_ref, k_ref, v_ref, lora_ref, wup_hi_ref, wup_lo_ref, aup_hi_ref, aup_lo_ref, w0_ref, a0_ref,
                    kk_g_ref, ka_ref, rk_ref, ones_ref, bonus_ref, y_ref, slot):
    n_b = r_ref.shape[0]
    ones = ones_ref[...]
    for b in range(n_b):
        r = r_ref[b].astype(F32)
        k = k_ref[b].astype(F32)
        v = v_ref[b].astype(F32)
        lora = lora_ref[b]
        wl = jnp.tanh(lora)
        kk = k * kk_g_ref[...]
        kk = kk * lax.rsqrt(jnp.maximum(_split_dot(kk * kk, ones), 1e-24))
        y_ref[slot["r"] + b] = r.T
        y_ref[slot["kk"] + b] = kk.T
        y_ref[slot["v"] + b] = v.T
        bsum = jnp.zeros_like(r)
        for e in range(2):
            w_pre = w0_ref[e:e + 1, :] + _split_dot(wl, wup_hi_ref[e], wup_lo_ref[e])
            y_ref[slot[f"w{e}"] + b] = jnp.exp(-jax.nn.sigmoid(w_pre) * math.exp(-0.5)).T
            a = jax.nn.sigmoid(a0_ref[e:e + 1, :] + _split_dot(lora, aup_hi_ref[e], aup_lo_ref[e]))
            kd = k * (1.0 + (a - 1.0) * ka_ref[...])
            y_ref[slot[f"kd{e}"] + b] = kd.T
            y_ref[slot[f"kka{e}"] + b] = (kk * a).T
            bsum = bsum + kd
        bonus_ref[b] = _split_dot(r * bsum * rk_ref[...], ones) * v


def _rwkv_prep(pa, lora, wup, aup, w0, a0, k_k, k_a, r_k, ones):
    B, N, _ = pa.shape
    blk = lambda c0: c0 // BRANCH_W
    tok_spec = lambda c0: pl.BlockSpec((B, XF_T, BRANCH_W), lambda i, g: (0, i, blk(c0)))
    const = lambda shape: pl.BlockSpec(shape, lambda i, g: (0,) * len(shape), pipeline_mode=pl.Buffered(1))
    split = lambda t: (t.astype(BF16), (t - t.astype(BF16).astype(F32)).astype(BF16))
    wup_hi, wup_lo = split(wup)
    aup_hi, aup_lo = split(aup)
    keyed_shape = jax.ShapeDtypeStruct((N // XF_T, RWKV_HEAD, XF_T, LANES), F32)
    keyed_spec = pl.BlockSpec((None, RWKV_HEAD // KEY_GROUPS, XF_T, LANES), lambda i, g: (i, g, 0, 0))
    return pl.pallas_call(
        _rwkv_prep_kernel,
        grid=(N // XF_T, KEY_GROUPS),
        in_specs=[
            tok_spec(A_RK_R), tok_spec(A_RK_K), tok_spec(A_RK_V),
            pl.BlockSpec((B, XF_T, LORA_PAD), lambda i, g: (0, i, 0)),
            const((2, LORA_PAD, BRANCH_W)), const((2, LORA_PAD, BRANCH_W)),
            const((2, LORA_PAD, BRANCH_W)), const((2, LORA_PAD, BRANCH_W)),
            const((2, BRANCH_W)), const((2, BRANCH_W)),
            const((1, BRANCH_W)), const((1, BRANCH_W)), const((1, BRANCH_W)),
            const((BRANCH_W, BRANCH_W)),
        ],
        out_specs=[keyed_spec] * len(KEYED) + [
            pl.BlockSpec((XF_T, SCAN_Q, LANES), lambda i, g: (i, 0, 0)),
            pl.BlockSpec((B, XF_T, BRANCH_W), lambda i, g: (0, i, 0))],
        out_shape=[keyed_shape] * len(KEYED) + [
            jax.ShapeDtypeStruct((N, SCAN_Q, LANES), F32),
            jax.ShapeDtypeStruct((B, N, BRANCH_W), F32)],
        scratch_shapes=[pltpu.VMEM(((len(KEYED) + 1) * B, BRANCH_W, XF_T), F32)],
        compiler_params=_cparams(("arbitrary", "arbitrary")),
        name="rwkv_prep",
    )(pa, pa, pa, lora, wup_hi, wup_lo, aup_hi, aup_lo, w0, a0,
      k_k.reshape(1, -1), k_a.reshape(1, -1), r_k.reshape(1, -1), ones)


def _rwkv_scan_kernel(wf, kkaf, kdf, wb, kkab, kdb, rf, kkf, rb, kkb, kkn_f, kkn_b, vf, vb,
                      of_ref, ob_ref, ckf, ckb, sf_ref, sb_ref, saf_ref, sab_ref):
    @pl.when(pl.program_id(0) == 0)
    def _():
        for ref in (sf_ref, sb_ref, saf_ref, sab_ref):
            ref[...] = jnp.zeros_like(ref)

    ckf[:, 0:SCAN_T, :] = kkf[...]
    ckf[:, SCAN_T:, :] = kkn_f[...]
    ckb[:, 0:SUBLANES, :] = kkn_b[...]
    ckb[:, SUBLANES:, :] = kkb[...]

    def chain(s_ref, sa, v_t, w, kka, kd, r, kk_next, t, t_next):
        o = jnp.zeros((SCAN_Q, LANES), F32)
        nsa = jnp.zeros((SCAN_Q, LANES), F32)
        row = pl.ds(t, 1)
        row_next = pl.ds(t_next, 1)
        for k in range(RWKV_HEAD):
            s = s_ref[k] * w[k, row, :] + sa * kka[k, row, :] + v_t * kd[k, row, :]
            s_ref[k] = s
            o = o + s * r[k, row, :]
            nsa = nsa + s * kk_next[k, row_next, :]
        return o, -nsa

    def step(t, carry):
        sa_f, sa_b = carry
        tb = SCAN_T - 1 - t
        o_f, sa_f = chain(sf_ref, sa_f, vf[t], wf, kkaf, kdf, rf, ckf, t, t + 1)
        o_b, sa_b = chain(sb_ref, sa_b, vb[tb], wb, kkab, kdb, rb, ckb, tb, tb + SUBLANES - 1)
        of_ref[t] = o_f
        ob_ref[tb] = o_b
        return sa_f, sa_b

    saf_ref[...], sab_ref[...] = lax.fori_loop(0, SCAN_T, step, (saf_ref[...], sab_ref[...]))


def _rwkv_scan(z, zv, n_ctx):
    N = zv.shape[0]
    nb, nc = N // SCAN_T, n_ctx // SCAN_T
    per_tile = XF_T // SCAN_T

    def mirror(i):
        return jnp.where(i < nc, nc - 1 - i, nb + nc - 1 - i)

    def tok_f(i):
        return jnp.minimum((i + 1) * SCAN_T, N - SUBLANES)

    def tok_b(i):
        return mirror(jnp.minimum(i + 1, nb - 1)) * SCAN_T + SCAN_T - 1

    def keyed(block_of):
        return pl.BlockSpec((None, RWKV_HEAD, SCAN_T, LANES),
                            lambda i: (block_of(i) // per_tile, 0, block_of(i) % per_tile, 0))

    def group(tok_of):
        return pl.BlockSpec((None, RWKV_HEAD, SUBLANES, LANES),
                            lambda i: (tok_of(i) // XF_T, 0, (tok_of(i) % XF_T) // SUBLANES, 0))

    fwd, bwd = keyed(lambda i: i), keyed(mirror)
    vf = pl.BlockSpec((SCAN_T, SCAN_Q, LANES), lambda i: (i, 0, 0))
    vb = pl.BlockSpec((SCAN_T, SCAN_Q, LANES), lambda i: (mirror(i), 0, 0))
    out = jax.ShapeDtypeStruct((N, SCAN_Q, LANES), F32)
    state = pltpu.VMEM((RWKV_HEAD, SCAN_Q, LANES), F32)
    return pl.pallas_call(
        _rwkv_scan_kernel,
        grid=(nb,),
        in_specs=[fwd] * 3 + [bwd] * 3 + [fwd, fwd, bwd, bwd, group(tok_f), group(tok_b), vf, vb],
        out_specs=[vf, vb],
        out_shape=[out, out],
        scratch_shapes=[pltpu.VMEM((RWKV_HEAD, SCAN_T + SUBLANES, LANES), F32)] * 2 + [state, state] + [
            pltpu.VMEM((SCAN_Q, LANES), F32)] * 2,
        compiler_params=_cparams(("arbitrary",)),
        name="rwkv_scan",
    )(z["w0"], z["kka0"], z["kd0"], z["w1"], z["kka1"], z["kd1"], z["r"], z["kk"], z["r"], z["kk"],
      z["kk"], z["kk"], zv, zv)


def _rwkv_post_kernel(of_ref, ob_ref, bonus_ref, gate_ref, g_ref, b_ref, ones_ref, o_ref, y_ref):
    n_b = bonus_ref.shape[0]
    copies = LANES // (n_b * RWKV_HEADS)
    for val in range(SCAN_Q):
        t = (of_ref[:, val, :] + ob_ref[:, val, :]).T
        for q in range(copies):
            for b in range(n_b):
                lo = (q * n_b + b) * RWKV_HEADS
                y_ref[b, pl.ds(q * SCAN_Q + val, RWKV_HEADS, stride=RWKV_HEAD), :] = t[lo:lo + RWKV_HEADS]
    ones = ones_ref[...]
    for b in range(n_b):
        y = y_ref[b].T
        mu = _split_dot(y, ones) * (1.0 / RWKV_HEAD)
        d = y - mu
        var = _split_dot(d * d, ones) * (1.0 / RWKV_HEAD)
        yn = d * lax.rsqrt(var + RWKV_GN_EPS) * g_ref[...] + b_ref[...]
        o_ref[b] = ((yn + bonus_ref[b]) * _silu(gate_ref[b].astype(F32))).astype(o_ref.dtype)


def _rwkv_post(o_f, o_b, bonus, pb, gn_g, gn_b, ones):
    B, N, _ = bonus.shape
    nat = pl.BlockSpec((B, XF_T, BRANCH_W), lambda i: (0, i, 0))
    scan = pl.BlockSpec((XF_T, SCAN_Q, LANES), lambda i: (i, 0, 0))
    const = lambda shape: pl.BlockSpec(shape, lambda i: (0, 0))
    return pl.pallas_call(
        _rwkv_post_kernel,
        grid=(N // XF_T,),
        in_specs=[scan, scan, nat,
                  pl.BlockSpec((B, XF_T, BRANCH_W), lambda i: (0, i, B_RK_G // BRANCH_W)),
                  const((1, BRANCH_W)), const((1, BRANCH_W)), const((BRANCH_W, BRANCH_W))],
        out_specs=nat,
        out_shape=jax.ShapeDtypeStruct((B, N, BRANCH_W), BF16),
        scratch_shapes=[pltpu.VMEM((B, BRANCH_W, XF_T), F32)],
        compiler_params=_cparams(("arbitrary",)),
        name="rwkv_post",
    )(o_f, o_b, bonus, pb, gn_g.reshape(1, -1), gn_b.reshape(1, -1), ones)


def _rwkv_branch(pa, pb, lora, li, p, ones, n_ctx):
    B, N, _ = pa.shape
    assert LANES % (B * RWKV_HEADS) == 0 and LANES // (B * RWKV_HEADS) * SCAN_Q == RWKV_HEAD
    assert n_ctx % SCAN_T == 0 and N % XF_T == 0 and XF_T % SCAN_T == 0
    outs = _rwkv_prep(pa, lora, p["wup"][li], p["aup"][li], p["rwkv_w0"][li], p["rwkv_a0"][li],
                      p["rwkv_k_k"][li], p["rwkv_k_a"][li], p["rwkv_r_k"][li], ones)
    z = dict(zip(KEYED, outs[:len(KEYED)]))
    zv, bonus = outs[len(KEYED):]
    o_f, o_b = _rwkv_scan(z, zv, n_ctx)
    return _rwkv_post(o_f, o_b, bonus, pb, p["rwkv_gn_g"][li], p["rwkv_gn_b"][li], ones)


def _merge_kernel(bc_ref, br_ref, bd_ref, bq_ref, lg_ref, wb_ref, bg_ref, wo_ref, x_ref, gp_ref, gt_ref,
                  sh_ref, sc_ref, gn_ref, o_ref, h_ref):
    merged = jnp.zeros((TOK, D_MODEL), F32)
    for j, b_ref in enumerate((bc_ref, br_ref, bd_ref, bq_ref)):
        pj = jnp.dot(b_ref[...], wb_ref[j], preferred_element_type=F32)
        gj = jax.nn.sigmoid(lg_ref[:, j * D_MODEL:(j + 1) * D_MODEL].astype(F32) + bg_ref[j:j + 1, :])
        merged = merged + pj * gj
    y = jnp.dot(merged.astype(BF16), wo_ref[...], preferred_element_type=F32)
    y = y * lax.rsqrt(jnp.mean(y * y, axis=-1, keepdims=True) + NORM_EPS) * gp_ref[...]
    x = x_ref[...] + gt_ref[...] * y
    o_ref[...] = x
    hn = x * lax.rsqrt(jnp.mean(x * x, axis=-1, keepdims=True) + NORM_EPS) * gn_ref[...]
    h_ref[...] = (hn * (1.0 + sc_ref[...]) + sh_ref[...]).astype(h_ref.dtype)


def _merge(branches, pc, w_branch, b_gate, w_out, li, xa, g_post, mod_l, mod_next, g_next):
    B, N, _ = xa.shape
    row = _mod_row(B)
    bspec = pl.BlockSpec((None, TOK, BRANCH_W), lambda b, i: (b, i, 0))
    xspec = pl.BlockSpec((None, TOK, D_MODEL), lambda b, i: (b, i, 0))
    mod_spec = lambda part: pl.BlockSpec((None, 1, D_MODEL), lambda b, i: row(b, i)[:2] + (part,))
    vec_spec = pl.BlockSpec((1, D_MODEL), lambda b, i: (0, 0))
    once = pl.Buffered(1)
    return pl.pallas_call(
        _merge_kernel,
        grid=(B, N // TOK),
        in_specs=[bspec] * 4 + [
            pl.BlockSpec((None, TOK, N_BRANCH * D_MODEL), lambda b, i: (b, i, 0)),
            pl.BlockSpec((None, N_BRANCH, BRANCH_W, D_MODEL), lambda b, i: (li, 0, 0, 0), pipeline_mode=once),
            pl.BlockSpec((None, N_BRANCH, D_MODEL), lambda b, i: (li, 0, 0)),
            pl.BlockSpec((None, D_MODEL, D_MODEL), lambda b, i: (li, 0, 0), pipeline_mode=once),
            xspec, vec_spec, mod_spec(2), mod_spec(0), mod_spec(1), vec_spec,
        ],
        out_specs=[xspec, xspec],
        out_shape=[jax.ShapeDtypeStruct((B, N, D_MODEL), F32), jax.ShapeDtypeStruct((B, N, D_MODEL), BF16)],
        compiler_params=_cparams(("arbitrary", "arbitrary")),
        name="merge",
    )(*branches, pc, w_branch, b_gate, w_out, xa, g_post.reshape(1, -1), mod_l, mod_next, mod_next,
      g_next.reshape(1, -1))


def kernel(x, c, ctx, c_ctx, norm_pre_g, norm_post_g, w_mod, b_mod, w_in, conv_w, conv_b, conv_ln_g, conv_ln_b,
           rwkv_w0, rwkv_w_up, rwkv_a0, rwkv_a_up, rwkv_k_k, rwkv_k_a, rwkv_r_k, rwkv_gn_g, rwkv_gn_b,
           diff_lam, diff_subln_g, gqa_qk_norm_g, w_branch, b_gate, w_out):
    B, n_lat, _ = x.shape
    n_ctx = ctx.shape[1]
    depth = w_in.shape[0]
    N = n_ctx + n_lat
    assert n_ctx == TOK and n_lat % TOK == 0 and B + 1 <= MOD_ROWS and w_in.shape[-1] == D_IN

    w_branch_b = w_branch.astype(BF16)
    w_out_b = w_out.astype(BF16)
    zrows = lambda t, lo: jnp.pad(t, ((0, 0), (0, 0), (lo, LORA_PAD - lo - t.shape[2]), (0, 0)))
    params = dict(wup=zrows(rwkv_w_up, 0), aup=zrows(rwkv_a_up, DECAY_LORA), rwkv_w0=rwkv_w0, rwkv_a0=rwkv_a0,
                  rwkv_k_k=rwkv_k_k, rwkv_k_a=rwkv_k_a, rwkv_r_k=rwkv_r_k.reshape(depth, BRANCH_W),
                  rwkv_gn_g=rwkv_gn_g, rwkv_gn_b=rwkv_gn_b)
    head = np.arange(BRANCH_W) // RWKV_HEAD
    ones = jnp.asarray(head[:, None] == head[None, :], BF16)
    cos_d, sin_d = _rope_tables(n_ctx, n_lat, DIFF_HALF, 2)
    cos_g, sin_g = _rope_tables(n_ctx, n_lat, GQA_HEAD, 1)

    cvec = jnp.concatenate([c, c_ctx[None, :], jnp.zeros((MOD_ROWS - B - 1, D_MODEL), c.dtype)], axis=0)
    mod = _modulation(cvec, w_mod, b_mod)
    xa = jnp.concatenate([ctx, x], axis=1)
    tm = 2176 if (B * N) % 2176 == 0 else TOK

    h = _prenorm(xa, mod[0], norm_pre_g[0])
    for li in range(depth):
        h = h.reshape(B * N, D_MODEL)
        pa = _project(h, w_in, li, O_A, O_LORA - O_A, 512, BF16, tm, (B, N, O_LORA - O_A))
        pb = _project(h, w_in, li, O_B, O_C - O_B, 384, BF16, tm, (B, N, O_C - O_B))
        pc = _project(h, w_in, li, O_C, D_IN - O_C, 512, BF16, tm, (B, N, D_IN - O_C))
        lora = _project(h, w_in, li, O_LORA, LORA_PAD, LORA_PAD, F32, tm, (B, N, LORA_PAD))
        lam_init = 0.8 - 0.6 * math.exp(-0.3 * li)
        br_conv = _conv_branch(pa, conv_w[li], conv_b[li], conv_ln_g[li], conv_ln_b[li])
        br_rwkv = _rwkv_branch(pa, pb, lora, li, params, ones, n_ctx)
        br_diff = _diff_branch(pb, diff_lam[li], diff_subln_g[li], cos_d, sin_d, n_ctx, lam_init)
        br_gqa = _gqa_branch(pb, gqa_qk_norm_g[li], cos_g, sin_g, n_ctx)
        nxt = min(li + 1, depth - 1)
        xa, h = _merge((br_conv, br_rwkv, br_diff, br_gqa), pc, w_branch_b, b_gate, w_out_b, li, xa,
                       norm_post_g[li], mod[li], mod[nxt], norm_pre_g[nxt])
    return xa[:, n_ctx:]
```

```python
import functools
import math

import jax
import jax.numpy as jnp
import numpy as np
from jax import lax
from jax.experimental import pallas as pl
from jax.experimental.pallas import tpu as pltpu

F32 = jnp.float32
BF16 = jnp.bfloat16

D_MODEL = 2048
BRANCH_W = D_MODEL // 2
N_BRANCH = 4
GRID_W = 64
CONV_K = 31
CONV_HALO = 16
RWKV_HEAD = 64
RWKV_HEADS = BRANCH_W // RWKV_HEAD
DECAY_LORA = 96
AAA_LORA = 96
LORA_PAD = 256
RWKV_GN_EPS = 64e-5
DIFF_HALF = 64
DIFF_VDIM = 2 * DIFF_HALF
DIFF_HEADS = BRANCH_W // DIFF_VDIM
GQA_HEAD = 128
GQA_HEADS = BRANCH_W // GQA_HEAD
GQA_KV_HEADS = GQA_HEADS // 4
GQA_REP = GQA_HEADS // GQA_KV_HEADS
ROPE_THETA = 10000.0
NORM_EPS = 1e-6

TOK = 256
LANES = 128
SUBLANES = 8
MOD_ROWS = 8
VMEM_LIMIT = 60 * 1024 * 1024

O_A = 0
O_LORA = 6 * BRANCH_W
O_B = O_LORA + DECAY_LORA + AAA_LORA
O_C = O_B + 6 * BRANCH_W + 2 * GQA_KV_HEADS * GQA_HEAD + BRANCH_W
D_IN = O_C + N_BRANCH * D_MODEL
A_CONV, A_RK_R, A_RK_K, A_RK_V = 0, 3 * BRANCH_W, 4 * BRANCH_W, 5 * BRANCH_W
B_RK_G, B_DF_Q, B_DF_K, B_DF_V, B_DF_G, B_GQ_Q = (j * BRANCH_W for j in range(6))
B_GQ_K = 6 * BRANCH_W
B_GQ_V = B_GQ_K + GQA_KV_HEADS * GQA_HEAD
B_GQ_G = B_GQ_V + GQA_KV_HEADS * GQA_HEAD


def _cparams(sem):
    return pltpu.CompilerParams(dimension_semantics=sem, vmem_limit_bytes=VMEM_LIMIT)


def _silu(t):
    return t * jax.nn.sigmoid(t)


def _split_dot(a, b_hi, b_lo=None):
    a_hi = a.astype(BF16)
    a_lo = (a - a_hi.astype(F32)).astype(BF16)
    out = jnp.dot(a_hi, b_hi, preferred_element_type=F32) + jnp.dot(a_lo, b_hi, preferred_element_type=F32)
    if b_lo is not None:
        out = out + jnp.dot(a_hi, b_lo, preferred_element_type=F32)
    return out


def _mod_kernel(c_ref, w_ref, b_ref, o_ref):
    cv = c_ref[...]
    s = _silu(cv).astype(BF16)
    o_ref[...] = jnp.dot(s, w_ref[...].astype(BF16), preferred_element_type=F32) + b_ref[...]


def _modulation(cvec, w_mod, b_mod):
    L = w_mod.shape[0]
    tn = 1536
    out = pl.pallas_call(
        _mod_kernel,
        grid=(L, 3 * D_MODEL // tn),
        in_specs=[
            pl.BlockSpec((MOD_ROWS, D_MODEL), lambda l, j: (0, 0)),
            pl.BlockSpec((None, D_MODEL, tn), lambda l, j: (l, 0, j)),
            pl.BlockSpec((None, 1, tn), lambda l, j: (l, 0, j)),
        ],
        out_specs=pl.BlockSpec((None, MOD_ROWS, tn), lambda l, j: (l, 0, j)),
        out_shape=jax.ShapeDtypeStruct((L, MOD_ROWS, 3 * D_MODEL), F32),
        compiler_params=_cparams(("arbitrary", "arbitrary")),
        name="modulation",
    )(cvec, w_mod, b_mod.reshape(L, 1, 3 * D_MODEL))
    return out.reshape(L, MOD_ROWS, 1, 3 * D_MODEL)


def _mod_row(batch):
    return lambda b, i: (jnp.where(i == 0, batch, b), 0, 0)


def _prenorm_kernel(x_ref, sh_ref, sc_ref, g_ref, h_ref):
    x = x_ref[...]
    y = x * lax.rsqrt(jnp.mean(x * x, axis=-1, keepdims=True) + NORM_EPS) * g_ref[...]
    h_ref[...] = (y * (1.0 + sc_ref[...]) + sh_ref[...]).astype(h_ref.dtype)


def _prenorm(xa, mod_l, g):
    B, N, _ = xa.shape
    row = _mod_row(B)
    return pl.pallas_call(
        _prenorm_kernel,
        grid=(B, N // TOK),
        in_specs=[
            pl.BlockSpec((None, TOK, D_MODEL), lambda b, i: (b, i, 0)),
            pl.BlockSpec((None, 1, D_MODEL), lambda b, i: row(b, i)[:2] + (0,)),
            pl.BlockSpec((None, 1, D_MODEL), lambda b, i: row(b, i)[:2] + (1,)),
            pl.BlockSpec((1, D_MODEL), lambda b, i: (0, 0)),
        ],
        out_specs=pl.BlockSpec((None, TOK, D_MODEL), lambda b, i: (b, i, 0)),
        out_shape=jax.ShapeDtypeStruct((B, N, D_MODEL), BF16),
        compiler_params=_cparams(("arbitrary", "arbitrary")),
        name="prenorm",
    )(xa, mod_l, mod_l, g.reshape(1, D_MODEL))


def _mm_kernel(a_ref, w_ref, o_ref):
    o_ref[...] = jnp.dot(a_ref[...], w_ref[...], preferred_element_type=F32).astype(o_ref.dtype)


def _mm_shift_kernel(a_ref, w_lo_ref, w_hi_ref, o_ref, w_ref, *, shift):
    @pl.when(pl.program_id(1) == 0)
    def _():
        w_ref[...] = jnp.concatenate([w_lo_ref[:, shift:], w_hi_ref[:, :shift]], axis=1)

    o_ref[...] = jnp.dot(a_ref[...], w_ref[...], preferred_element_type=F32).astype(o_ref.dtype)


def _project(h, w_all, li, col0, ncols, tn, out_dtype, tm, shape3):
    M, K = h.shape
    first, shift = divmod(col0, tn)
    w_spec = lambda off: pl.BlockSpec((None, K, tn), lambda j, i: (li, 0, first + off + j))
    if shift == 0:
        body, w_specs, scratch = _mm_kernel, [w_spec(0)], []
    else:
        body = functools.partial(_mm_shift_kernel, shift=shift)
        w_specs, scratch = [w_spec(0), w_spec(1)], [pltpu.VMEM((K, tn), w_all.dtype)]
    out = pl.pallas_call(
        body,
        grid=(ncols // tn, M // tm),
        in_specs=[pl.BlockSpec((tm, K), lambda j, i: (i, 0))] + w_specs,
        out_specs=pl.BlockSpec((tm, tn), lambda j, i: (i, j)),
        out_shape=jax.ShapeDtypeStruct((M, ncols), out_dtype),
        scratch_shapes=scratch,
        compiler_params=_cparams(("arbitrary", "arbitrary")),
        name="in_proj",
    )(h, *([w_all] * len(w_specs)))
    return out.reshape(shape3)


def _conv_kernel(cur_ref, gate_ref, prev_ref, next_ref, cw_ref, cb_ref, lg_ref, lb_ref, o_ref, buf_ref, *, n_tiles):
    i = pl.program_id(1)

    def glu(t):
        return t[:, :BRANCH_W].astype(F32) * jax.nn.sigmoid(t[:, BRANCH_W:2 * BRANCH_W].astype(F32))

    has_prev = jnp.logical_and(i != 0, i != 1)
    has_next = jnp.logical_and(i != 0, i != n_tiles - 1)
    cur = cur_ref[...]
    buf_ref[0:CONV_HALO, :] = jnp.where(has_prev, glu(prev_ref[...]), 0.0)
    buf_ref[CONV_HALO:CONV_HALO + TOK, :] = glu(cur)
    buf_ref[CONV_HALO + TOK:, :] = jnp.where(has_next, glu(next_ref[...]), 0.0)
    base = CONV_HALO - CONV_K // 2
    acc = jnp.broadcast_to(cb_ref[...], (TOK, BRANCH_W))
    for r in range(SUBLANES):
        part = None
        for j in range(CONV_K):
            if (base + j) % SUBLANES == r:
                lo = base + j - r
                term = cw_ref[j:j + 1, :] * buf_ref[lo:lo + TOK + SUBLANES, :]
                part = term if part is None else part + term
        if part is not None:
            acc = acc + part[r:r + TOK]
    mu = jnp.mean(acc, axis=-1, keepdims=True)
    d = acc - mu
    var = jnp.mean(d * d, axis=-1, keepdims=True)
    y = d * lax.rsqrt(var + 1e-5) * lg_ref[...] + lb_ref[...]
    o_ref[...] = (_silu(y) * _silu(gate_ref[...].astype(F32))).astype(o_ref.dtype)


def _conv_branch(pa, conv_w, conv_b, ln_g, ln_b):
    B, N, _ = pa.shape
    nt = N // TOK
    hpt = TOK // CONV_HALO
    last = N // CONV_HALO - 1
    cblk = A_CONV // (2 * BRANCH_W)
    return pl.pallas_call(
        functools.partial(_conv_kernel, n_tiles=nt),
        grid=(B, nt),
        in_specs=[
            pl.BlockSpec((None, TOK, 2 * BRANCH_W), lambda b, i: (b, i, cblk)),
            pl.BlockSpec((None, TOK, BRANCH_W), lambda b, i: (b, i, 2 * cblk + 2)),
            pl.BlockSpec((None, CONV_HALO, 2 * BRANCH_W), lambda b, i: (b, jnp.maximum(i * hpt - 1, 0), cblk)),
            pl.BlockSpec((None, CONV_HALO, 2 * BRANCH_W), lambda b, i: (b, jnp.minimum((i + 1) * hpt, last), cblk)),
            pl.BlockSpec((CONV_K, BRANCH_W), lambda b, i: (0, 0)),
            pl.BlockSpec((1, BRANCH_W), lambda b, i: (0, 0)),
            pl.BlockSpec((1, BRANCH_W), lambda b, i: (0, 0)),
            pl.BlockSpec((1, BRANCH_W), lambda b, i: (0, 0)),
        ],
        out_specs=pl.BlockSpec((None, TOK, BRANCH_W), lambda b, i: (b, i, 0)),
        out_shape=jax.ShapeDtypeStruct((B, N, BRANCH_W), BF16),
        scratch_shapes=[pltpu.VMEM((TOK + 2 * CONV_HALO, BRANCH_W), F32)],
        compiler_params=_cparams(("arbitrary", "arbitrary")),
        name="conv_branch",
    )(pa, pa, pa, pa, conv_w, conv_b.reshape(1, -1), ln_g.reshape(1, -1), ln_b.reshape(1, -1))


def _rope_tables(n_ctx, n_lat, dim, reps):
    half = dim // 2
    t = np.arange(n_lat)
    inv = ROPE_THETA ** (-np.arange(0, half, 2, dtype=np.float64) / half)
    ang = np.concatenate([(t // GRID_W)[:, None] * inv, (t % GRID_W)[:, None] * inv], axis=-1)
    ang = np.concatenate([np.zeros((n_ctx, half)), ang], axis=0)
    cos = np.repeat(np.cos(ang), 2, axis=-1)
    sin = np.repeat(np.sin(ang), 2, axis=-1) * np.tile(np.array([-1.0, 1.0]), half)
    return (jnp.asarray(np.tile(cos, (1, reps)), F32), jnp.asarray(np.tile(sin, (1, reps)), F32))


def _rope(t, cos, sin_signed):
    lane = lax.broadcasted_iota(jnp.int32, t.shape, 1)
    partner = jnp.where(lane % 2 == 0, pltpu.roll(t, LANES - 1, 1), pltpu.roll(t, 1, 1))
    return t * cos + partner * sin_signed


LOG2_E = 1.4426950408889634


ATTN_CHUNK = 1280


def _scores_t(k, q):
    return lax.dot_general(k, q, (((1,), (1,)), ((), ())), preferred_element_type=F32)


def _online_attention(kp_ref, vt_ref, qs, nk):
    chunks = [(lo, min(ATTN_CHUNK, nk - lo)) for lo in range(0, nk, ATTN_CHUNK)]

    def scores(c):
        lo, n = chunks[c]
        k = kp_ref[lo:lo + n, :]
        return [_scores_t(k, q) for q in qs]

    state = [None] * len(qs)
    s_next = scores(0)
    for c, (lo, n) in enumerate(chunks):
        s_cur = s_next
        if c + 1 < len(chunks):
            s_next = scores(c + 1)
        v = vt_ref[:, lo:lo + n]
        for j, s in enumerate(s_cur):
            m = jnp.max(s, axis=0, keepdims=True)
            if state[j] is not None:
                m_old, l_old, acc_old = state[j]
                m = jnp.maximum(m_old, m)
            e = jnp.exp2(s - m)
            l = jnp.sum(e, axis=0, keepdims=True)
            acc = jnp.dot(v, e.astype(BF16), preferred_element_type=F32)
            if state[j] is not None:
                alpha = jnp.exp2(m_old - m)
                l = l_old * alpha + l
                acc = acc_old * alpha + acc
            state[j] = (m, l, acc)
    return [(acc, l) for _, l, acc in state]


def _gqa_kernel(q_ref, k_ref, v_ref, gate_ref, cos_ref, sin_ref, g_ref, o_ref, kp_ref, vt_ref, *, n_ctx, n_all):
    r = pl.program_id(2)
    qi = pl.program_id(3)

    @pl.when(jnp.logical_and(r == 0, qi == 0))
    def _():
        k = k_ref[...].astype(F32)
        k = k * lax.rsqrt(jnp.mean(k * k, axis=-1, keepdims=True) + NORM_EPS) * g_ref[1:2, :]
        kp_ref[...] = _rope(k, cos_ref[...], sin_ref[...]).astype(BF16)
        vt_ref[...] = v_ref[...].astype(F32).T.astype(BF16)

    rows = pl.ds(pl.multiple_of(qi * TOK, TOK), TOK)
    qs = []
    for j in range(GQA_PAIR):
        q = q_ref[:, j * GQA_HEAD:(j + 1) * GQA_HEAD].astype(F32)
        q = q * lax.rsqrt(jnp.mean(q * q, axis=-1, keepdims=True) + NORM_EPS) * g_ref[0:1, :]
        q = _rope(q, cos_ref[rows, :], sin_ref[rows, :]) * (GQA_HEAD ** -0.5 * LOG2_E)
        qs.append(q.astype(BF16))
    gate = _silu(gate_ref[...].astype(F32))

    def attend(nk):
        o = jnp.concatenate([(acc / l).T for acc, l in _online_attention(kp_ref, vt_ref, qs, nk)], axis=1)
        o_ref[...] = (o * gate).astype(o_ref.dtype)

    pl.when(qi == 0)(lambda: attend(n_ctx))
    pl.when(qi != 0)(lambda: attend(n_all))


GQA_PAIR = 2


def _gqa_branch(pb, qk_g, cos, sin, n_ctx):
    B, N, _ = pb.shape
    blk = lambda c0: c0 // GQA_HEAD
    wide = GQA_PAIR * GQA_HEAD
    steps = GQA_REP // GQA_PAIR
    return pl.pallas_call(
        functools.partial(_gqa_kernel, n_ctx=n_ctx, n_all=N),
        grid=(B, GQA_KV_HEADS, steps, N // TOK),
        in_specs=[
            pl.BlockSpec((None, TOK, wide), lambda b, g, r, i: (b, i, B_GQ_Q // wide + g * steps + r)),
            pl.BlockSpec((None, N, GQA_HEAD), lambda b, g, r, i: (b, 0, blk(B_GQ_K) + g)),
            pl.BlockSpec((None, N, GQA_HEAD), lambda b, g, r, i: (b, 0, blk(B_GQ_V) + g)),
            pl.BlockSpec((None, TOK, wide), lambda b, g, r, i: (b, i, B_GQ_G // wide + g * steps + r)),
            pl.BlockSpec((N, GQA_HEAD), lambda b, g, r, i: (0, 0)),
            pl.BlockSpec((N, GQA_HEAD), lambda b, g, r, i: (0, 0)),
            pl.BlockSpec((2, GQA_HEAD), lambda b, g, r, i: (0, 0)),
        ],
        out_specs=pl.BlockSpec((None, TOK, wide), lambda b, g, r, i: (b, i, g * steps + r)),
        out_shape=jax.ShapeDtypeStruct((B, N, BRANCH_W), BF16),
        scratch_shapes=[pltpu.VMEM((N, GQA_HEAD), BF16), pltpu.VMEM((GQA_HEAD, N), BF16)],
        compiler_params=_cparams(("arbitrary",) * 4),
        name="gqa_branch",
    )(pb, pb, pb, pb, cos, sin, qk_g)


def _diff_kernel(q_ref, k_ref, v_ref, gate_ref, cos_ref, sin_ref, lam_ref, sg_ref, o_ref, kp_ref, vt_ref,
                 *, n_ctx, n_all, lam_init):
    qi = pl.program_id(2)

    @pl.when(qi == 0)
    def _():
        kp_ref[...] = _rope(k_ref[...].astype(F32), cos_ref[...], sin_ref[...]).astype(BF16)
        vt_ref[...] = v_ref[...].astype(F32).T.astype(BF16)

    lp = lam_ref[...]
    lam = (jnp.exp(jnp.sum(lp[0:1] * lp[1:2], axis=-1, keepdims=True))
           - jnp.exp(jnp.sum(lp[2:3] * lp[3:4], axis=-1, keepdims=True)) + lam_init)
    rows = pl.ds(pl.multiple_of(qi * TOK, TOK), TOK)
    q = _rope(q_ref[...].astype(F32), cos_ref[rows, :], sin_ref[rows, :]) * (DIFF_HALF ** -0.5 * LOG2_E)
    lane = lax.broadcasted_iota(jnp.int32, q.shape, 1)
    q1 = jnp.where(lane < DIFF_HALF, q, 0.0).astype(BF16)
    q2 = jnp.where(lane >= DIFF_HALF, q, 0.0).astype(BF16)
    gate = _silu(gate_ref[...].astype(F32))

    def attend(nk):
        (acc1, l1), (acc2, l2) = _online_attention(kp_ref, vt_ref, (q1, q2), nk)
        o = (acc1 / l1 - acc2 * (lam / l2)).T
        o = o * lax.rsqrt(jnp.mean(o * o, axis=-1, keepdims=True) + NORM_EPS) * sg_ref[...]
        o_ref[...] = (o * (1.0 - lam_init) * gate).astype(o_ref.dtype)

    pl.when(qi == 0)(lambda: attend(n_ctx))
    pl.when(qi != 0)(lambda: attend(n_all))


def _diff_branch(pb, lam_p, subln_g, cos, sin, n_ctx, lam_init):
    B, N, _ = pb.shape
    blk = lambda c0: c0 // DIFF_VDIM
    return pl.pallas_call(
        functools.partial(_diff_kernel, n_ctx=n_ctx, n_all=N, lam_init=lam_init),
        grid=(B, DIFF_HEADS, N // TOK),
        in_specs=[
            pl.BlockSpec((None, TOK, DIFF_VDIM), lambda b, h, i: (b, i, blk(B_DF_Q) + h)),
            pl.BlockSpec((None, N, DIFF_VDIM), lambda b, h, i: (b, 0, blk(B_DF_K) + h)),
            pl.BlockSpec((None, N, DIFF_VDIM), lambda b, h, i: (b, 0, blk(B_DF_V) + h)),
            pl.BlockSpec((None, TOK, DIFF_VDIM), lambda b, h, i: (b, i, blk(B_DF_G) + h)),
            pl.BlockSpec((N, DIFF_VDIM), lambda b, h, i: (0, 0)),
            pl.BlockSpec((N, DIFF_VDIM), lambda b, h, i: (0, 0)),
            pl.BlockSpec((4, DIFF_HALF), lambda b, h, i: (0, 0)),
            pl.BlockSpec((1, DIFF_VDIM), lambda b, h, i: (0, 0)),
        ],
        out_specs=pl.BlockSpec((None, TOK, DIFF_VDIM), lambda b, h, i: (b, i, h)),
        out_shape=jax.ShapeDtypeStruct((B, N, BRANCH_W), BF16),
        scratch_shapes=[pltpu.VMEM((N, DIFF_VDIM), BF16), pltpu.VMEM((DIFF_VDIM, N), BF16)],
        compiler_params=_cparams(("arbitrary",) * 3),
        name="diff_branch",
    )(pb, pb, pb, pb, cos, sin, lam_p, subln_g.reshape(1, DIFF_VDIM))


XF_T = 128
SCAN_T = 64
SCAN_Q = RWKV_HEAD // 4
KEYED = ("w0", "kka0", "kd0", "w1", "kka1", "kd1", "r", "kk")
KEY_GROUPS = 4


def _rwkv_prep_kernel(r_ref, k_ref, v_ref, lora_ref, wup_hi_ref, wup_lo_ref, aup_hi_ref, aup_lo_ref, w0_ref, a0_ref,
                      kk_g_ref, ka_ref, rk_ref, ones_ref, *refs):
    out = dict(zip(KEYED, refs[:len(KEYED)]))
    zv_ref, bonus_ref, y_ref = refs[len(KEYED):]
    n_b = r_ref.shape[0]
    slot = {name: j * n_b for j, name in enumerate(KEYED + ("v",))}
    copies = LANES // (n_b * RWKV_HEADS)
    group = pl.program_id(1)

    @pl.when(group == 0)
    def _():
        _rwkv_prep_tile(r_ref, k_ref, v_ref, lora_ref, wup_hi_ref, wup_lo_ref, aup_hi_ref, aup_lo_ref, w0_ref, a0_ref,
                        kk_g_ref, ka_ref, rk_ref, ones_ref, bonus_ref, y_ref, slot)
        for val in range(SCAN_Q):
            rows = [y_ref[slot["v"] + b, pl.ds(q * SCAN_Q + val, RWKV_HEADS, stride=RWKV_HEAD), :]
                    for q in range(copies) for b in range(n_b)]
            zv_ref[:, val, :] = jnp.concatenate(rows, axis=0).T

    for g in range(KEY_GROUPS):
        @pl.when(group == g)
        def _(g=g):
            for name in KEYED:
                for j in range(RWKV_HEAD // KEY_GROUPS):
                    key = g * (RWKV_HEAD // KEY_GROUPS) + j
                    rows = [y_ref[slot[name] + b, pl.ds(key, RWKV_HEADS, stride=RWKV_HEAD), :] for b in range(n_b)]
                    out[name][j] = jnp.concatenate(rows * copies, axis=0).T


def _rwkv_prep_tile(r_ref, k_ref, v_ref, lora_ref, wup_hi_ref, wup_lo_ref, aup_hi_ref, aup_lo_ref, w0_ref, a0_ref,
                    kk_g_ref, ka_ref, rk_ref, ones_ref, bonus_ref, y_ref, slot):
    n_b = r_ref.shape[0]
    ones = ones_ref[...]
    for b in range(n_b):
        r = r_ref[b].astype(F32)
        k = k_ref[b].astype(F32)
        v = v_ref[b].astype(F32)
        lora = lora_ref[b]
        wl = jnp.tanh(lora)
        kk = k * kk_g_ref[...]
        kk = kk * lax.rsqrt(jnp.maximum(_split_dot(kk * kk, ones), 1e-24))
        y_ref[slot["r"] + b] = r.T
        y_ref[slot["kk"] + b] = kk.T
        y_ref[slot["v"] + b] = v.T
        bsum = jnp.zeros_like(r)
        for e in range(2):
            w_pre = w0_ref[e:e + 1, :] + _split_dot(wl, wup_hi_ref[e], wup_lo_ref[e])
            y_ref[slot[f"w{e}"] + b] = jnp.exp(-jax.nn.sigmoid(w_pre) * math.exp(-0.5)).T
            a = jax.nn.sigmoid(a0_ref[e:e + 1, :] + _split_dot(lora, aup_hi_ref[e], aup_lo_ref[e]))
            kd = k * (1.0 + (a - 1.0) * ka_ref[...])
            y_ref[slot[f"kd{e}"] + b] = kd.T
            y_ref[slot[f"kka{e}"] + b] = (kk * a).T
            bsum = bsum + kd
        bonus_ref[b] = _split_dot(r * bsum * rk_ref[...], ones) * v


def _rwkv_prep(pa, lora, wup, aup, w0, a0, k_k, k_a, r_k, ones):
    B, N, _ = pa.shape
    blk = lambda c0: c0 // BRANCH_W
    tok_spec = lambda c0: pl.BlockSpec((B, XF_T, BRANCH_W), lambda i, g: (0, i, blk(c0)))
    const = lambda shape: pl.BlockSpec(shape, lambda i, g: (0,) * len(shape), pipeline_mode=pl.Buffered(1))
    split = lambda t: (t.astype(BF16), (t - t.astype(BF16).astype(F32)).astype(BF16))
    wup_hi, wup_lo = split(wup)
    aup_hi, aup_lo = split(aup)
    keyed_shape = jax.ShapeDtypeStruct((N // XF_T, RWKV_HEAD, XF_T, LANES), F32)
    keyed_spec = pl.BlockSpec((None, RWKV_HEAD // KEY_GROUPS, XF_T, LANES), lambda i, g: (i, g, 0, 0))
    return pl.pallas_call(
        _rwkv_prep_kernel,
        grid=(N // XF_T, KEY_GROUPS),
        in_specs=[
            tok_spec(A_RK_R), tok_spec(A_RK_K), tok_spec(A_RK_V),
            pl.BlockSpec((B, XF_T, LORA_PAD), lambda i, g: (0, i, 0)),
            const((2, LORA_PAD, BRANCH_W)), const((2, LORA_PAD, BRANCH_W)),
            const((2, LORA_PAD, BRANCH_W)), const((2, LORA_PAD, BRANCH_W)),
            const((2, BRANCH_W)), const((2, BRANCH_W)),
            const((1, BRANCH_W)), const((1, BRANCH_W)), const((1, BRANCH_W)),
            const((BRANCH_W, BRANCH_W)),
        ],
        out_specs=[keyed_spec] * len(KEYED) + [
            pl.BlockSpec((XF_T, SCAN_Q, LANES), lambda i, g: (i, 0, 0)),
            pl.BlockSpec((B, XF_T, BRANCH_W), lambda i, g: (0, i, 0))],
        out_shape=[keyed_shape] * len(KEYED) + [
            jax.ShapeDtypeStruct((N, SCAN_Q, LANES), F32),
            jax.ShapeDtypeStruct((B, N, BRANCH_W), F32)],
        scratch_shapes=[pltpu.VMEM(((len(KEYED) + 1) * B, BRANCH_W, XF_T), F32)],
        compiler_params=_cparams(("arbitrary", "arbitrary")),
        name="rwkv_prep",
    )(pa, pa, pa, lora, wup_hi, wup_lo, aup_hi, aup_lo, w0, a0,
      k_k.reshape(1, -1), k_a.reshape(1, -1), r_k.reshape(1, -1), ones)


def _rwkv_scan_kernel(wf, kkaf, kdf, wb, kkab, kdb, rf, kkf, rb, kkb, kkn_f, kkn_b, vf, vb,
                      of_ref, ob_ref, ckf, ckb, sf_ref, sb_ref, saf_ref, sab_ref):
    @pl.when(pl.program_id(0) == 0)
    def _():
        for ref in (sf_ref, sb_ref, saf_ref, sab_ref):
            ref[...] = jnp.zeros_like(ref)

    ckf[:, 0:SCAN_T, :] = kkf[...]
    ckf[:, SCAN_T:, :] = kkn_f[...]
    ckb[:, 0:SUBLANES, :] = kkn_b[...]
    ckb[:, SUBLANES:, :] = kkb[...]

    def chain(s_ref, sa, v_t, w, kka, kd, r, kk_next, t, t_next):
        o = jnp.zeros((SCAN_Q, LANES), F32)
        nsa = jnp.zeros((SCAN_Q, LANES), F32)
        row = pl.ds(t, 1)
        row_next = pl.ds(t_next, 1)
        for k in range(RWKV_HEAD):
            s = s_ref[k] * w[k, row, :] + sa * kka[k, row, :] + v_t * kd[k, row, :]
            s_ref[k] = s
            o = o + s * r[k, row, :]
            nsa = nsa + s * kk_next[k, row_next, :]
        return o, -nsa

    def step(t, carry):
        sa_f, sa_b = carry
        tb = SCAN_T - 1 - t
        o_f, sa_f = chain(sf_ref, sa_f, vf[t], wf, kkaf, kdf, rf, ckf, t, t + 1)
        o_b, sa_b = chain(sb_ref, sa_b, vb[tb], wb, kkab, kdb, rb, ckb, tb, tb + SUBLANES - 1)
        of_ref[t] = o_f
        ob_ref[tb] = o_b
        return sa_f, sa_b

    saf_ref[...], sab_ref[...] = lax.fori_loop(0, SCAN_T, step, (saf_ref[...], sab_ref[...]))


def _rwkv_scan(z, zv, n_ctx):
    N = zv.shape[0]
    nb, nc = N // SCAN_T, n_ctx // SCAN_T
    per_tile = XF_T // SCAN_T

    def mirror(i):
        return jnp.where(i < nc, nc - 1 - i, nb + nc - 1 - i)

    def tok_f(i):
        return jnp.minimum((i + 1) * SCAN_T, N - SUBLANES)

    def tok_b(i):
        return mirror(jnp.minimum(i + 1, nb - 1)) * SCAN_T + SCAN_T - 1

    def keyed(block_of):
        return pl.BlockSpec((None, RWKV_HEAD, SCAN_T, LANES),
                            lambda i: (block_of(i) // per_tile, 0, block_of(i) % per_tile, 0))

    def group(tok_of):
        return pl.BlockSpec((None, RWKV_HEAD, SUBLANES, LANES),
                            lambda i: (tok_of(i) // XF_T, 0, (tok_of(i) % XF_T) // SUBLANES, 0))

    fwd, bwd = keyed(lambda i: i), keyed(mirror)
    vf = pl.BlockSpec((SCAN_T, SCAN_Q, LANES), lambda i: (i, 0, 0))
    vb = pl.BlockSpec((SCAN_T, SCAN_Q, LANES), lambda i: (mirror(i), 0, 0))
    out = jax.ShapeDtypeStruct((N, SCAN_Q, LANES), F32)
    state = pltpu.VMEM((RWKV_HEAD, SCAN_Q, LANES), F32)
    return pl.pallas_call(
        _rwkv_scan_kernel,
        grid=(nb,),
        in_specs=[fwd] * 3 + [bwd] * 3 + [fwd, fwd, bwd, bwd, group(tok_f), group(tok_b), vf, vb],
        out_specs=[vf, vb],
        out_shape=[out, out],
        scratch_shapes=[pltpu.VMEM((RWKV_HEAD, SCAN_T + SUBLANES, LANES), F32)] * 2 + [state, state] + [
            pltpu.VMEM((SCAN_Q, LANES), F32)] * 2,
        compiler_params=_cparams(("arbitrary",)),
        name="rwkv_scan",
    )(z["w0"], z["kka0"], z["kd0"], z["w1"], z["kka1"], z["kd1"], z["r"], z["kk"], z["r"], z["kk"],
      z["kk"], z["kk"], zv, zv)


def _rwkv_post_kernel(of_ref, ob_ref, bonus_ref, gate_ref, g_ref, b_ref, ones_ref, o_ref, y_ref):
    n_b = bonus_ref.shape[0]
    copies = LANES // (n_b * RWKV_HEADS)
    for val in range(SCAN_Q):
        t = (of_ref[:, val, :] + ob_ref[:, val, :]).T
        for q in range(copies):
            for b in range(n_b):
                lo = (q * n_b + b) * RWKV_HEADS
                y_ref[b, pl.ds(q * SCAN_Q + val, RWKV_HEADS, stride=RWKV_HEAD), :] = t[lo:lo + RWKV_HEADS]
    ones = ones_ref[...]
    for b in range(n_b):
        y = y_ref[b].T
        mu = _split_dot(y, ones) * (1.0 / RWKV_HEAD)
        d = y - mu
        var = _split_dot(d * d, ones) * (1.0 / RWKV_HEAD)
        yn = d * lax.rsqrt(var + RWKV_GN_EPS) * g_ref[...] + b_ref[...]
        o_ref[b] = ((yn + bonus_ref[b]) * _silu(gate_ref[b].astype(F32))).astype(o_ref.dtype)


def _rwkv_post(o_f, o_b, bonus, pb, gn_g, gn_b, ones):
    B, N, _ = bonus.shape
    nat = pl.BlockSpec((B, XF_T, BRANCH_W), lambda i: (0, i, 0))
    scan = pl.BlockSpec((XF_T, SCAN_Q, LANES), lambda i: (i, 0, 0))
    const = lambda shape: pl.BlockSpec(shape, lambda i: (0, 0))
    return pl.pallas_call(
        _rwkv_post_kernel,
        grid=(N // XF_T,),
        in_specs=[scan, scan, nat,
                  pl.BlockSpec((B, XF_T, BRANCH_W), lambda i: (0, i, B_RK_G // BRANCH_W)),
                  const((1, BRANCH_W)), const((1, BRANCH_W)), const((BRANCH_W, BRANCH_W))],
        out_specs=nat,
        out_shape=jax.ShapeDtypeStruct((B, N, BRANCH_W), BF16),
        scratch_shapes=[pltpu.VMEM((B, BRANCH_W, XF_T), F32)],
        compiler_params=_cparams(("arbitrary",)),
        name="rwkv_post",
    )(o_f, o_b, bonus, pb, gn_g.reshape(1, -1), gn_b.reshape(1, -1), ones)


def _rwkv_branch(pa, pb, lora, li, p, ones, n_ctx):
    B, N, _ = pa.shape
    assert LANES % (B * RWKV_HEADS) == 0 and LANES // (B * RWKV_HEADS) * SCAN_Q == RWKV_HEAD
    assert n_ctx % SCAN_T == 0 and N % XF_T == 0 and XF_T % SCAN_T == 0
    outs = _rwkv_prep(pa, lora, p["wup"][li], p["aup"][li], p["rwkv_w0"][li], p["rwkv_a0"][li],
                      p["rwkv_k_k"][li], p["rwkv_k_a"][li], p["rwkv_r_k"][li], ones)
    z = dict(zip(KEYED, outs[:len(KEYED)]))
    zv, bonus = outs[len(KEYED):]
    o_f, o_b = _rwkv_scan(z, zv, n_ctx)
    return _rwkv_post(o_f, o_b, bonus, pb, p["rwkv_gn_g"][li], p["rwkv_gn_b"][li], ones)


def _merge_kernel(bc_ref, br_ref, bd_ref, bq_ref, lg_ref, wb_ref, bg_ref, wo_ref, x_ref, gp_ref, gt_ref,
                  sh_ref, sc_ref, gn_ref, o_ref, h_ref):
    merged = jnp.zeros((TOK, D_MODEL), F32)
    for j, b_ref in enumerate((bc_ref, br_ref, bd_ref, bq_ref)):
        pj = jnp.dot(b_ref[...], wb_ref[j], preferred_element_type=F32)
        gj = jax.nn.sigmoid(lg_ref[:, j * D_MODEL:(j + 1) * D_MODEL].astype(F32) + bg_ref[j:j + 1, :])
        merged = merged + pj * gj
    y = jnp.dot(merged.astype(BF16), wo_ref[...], preferred_element_type=F32)
    y = y * lax.rsqrt(jnp.mean(y * y, axis=-1, keepdims=True) + NORM_EPS) * gp_ref[...]
    x = x_ref[...] + gt_ref[...] * y
    o_ref[...] = x
    hn = x * lax.rsqrt(jnp.mean(x * x, axis=-1, keepdims=True) + NORM_EPS) * gn_ref[...]
    h_ref[...] = (hn * (1.0 + sc_ref[...]) + sh_ref[...]).astype(h_ref.dtype)


def _merge(branches, pc, w_branch, b_gate, w_out, li, xa, g_post, mod_l, mod_next, g_next):
    B, N, _ = xa.shape
    row = _mod_row(B)
    bspec = pl.BlockSpec((None, TOK, BRANCH_W), lambda b, i: (b, i, 0))
    xspec = pl.BlockSpec((None, TOK, D_MODEL), lambda b, i: (b, i, 0))
    mod_spec = lambda part: pl.BlockSpec((None, 1, D_MODEL), lambda b, i: row(b, i)[:2] + (part,))
    vec_spec = pl.BlockSpec((1, D_MODEL), lambda b, i: (0, 0))
    once = pl.Buffered(1)
    return pl.pallas_call(
        _merge_kernel,
        grid=(B, N // TOK),
        in_specs=[bspec] * 4 + [
            pl.BlockSpec((None, TOK, N_BRANCH * D_MODEL), lambda b, i: (b, i, 0)),
            pl.BlockSpec((None, N_BRANCH, BRANCH_W, D_MODEL), lambda b, i: (li, 0, 0, 0), pipeline_mode=once),
            pl.BlockSpec((None, N_BRANCH, D_MODEL), lambda b, i: (li, 0, 0)),
            pl.BlockSpec((None, D_MODEL, D_MODEL), lambda b, i: (li, 0, 0), pipeline_mode=once),
            xspec, vec_spec, mod_spec(2), mod_spec(0), mod_spec(1), vec_spec,
        ],
        out_specs=[xspec, xspec],
        out_shape=[jax.ShapeDtypeStruct((B, N, D_MODEL), F32), jax.ShapeDtypeStruct((B, N, D_MODEL), BF16)],
        compiler_params=_cparams(("arbitrary", "arbitrary")),
        name="merge",
    )(*branches, pc, w_branch, b_gate, w_out, xa, g_post.reshape(1, -1), mod_l, mod_next, mod_next,
      g_next.reshape(1, -1))


def kernel(x, c, ctx, c_ctx, norm_pre_g, norm_post_g, w_mod, b_mod, w_in, conv_w, conv_b, conv_ln_g, conv_ln_b,
           rwkv_w0, rwkv_w_up, rwkv_a0, rwkv_a_up, rwkv_k_k, rwkv_k_a, rwkv_r_k, rwkv_gn_g, rwkv_gn_b,
           diff_lam, diff_subln_g, gqa_qk_norm_g, w_branch, b_gate, w_out):
    B, n_lat, _ = x.shape
    n_ctx = ctx.shape[1]
    depth = w_in.shape[0]
    N = n_ctx + n_lat
    assert n_ctx == TOK and n_lat % TOK == 0 and B + 1 <= MOD_ROWS and w_in.shape[-1] == D_IN

    w_in_b = w_in.astype(BF16)
    w_branch_b = w_branch.astype(BF16)
    w_out_b = w_out.astype(BF16)
    zrows = lambda t, lo: jnp.pad(t, ((0, 0), (0, 0), (lo, LORA_PAD - lo - t.shape[2]), (0, 0)))
    params = dict(wup=zrows(rwkv_w_up, 0), aup=zrows(rwkv_a_up, DECAY_LORA), rwkv_w0=rwkv_w0, rwkv_a0=rwkv_a0,
                  rwkv_k_k=rwkv_k_k, rwkv_k_a=rwkv_k_a, rwkv_r_k=rwkv_r_k.reshape(depth, BRANCH_W),
                  rwkv_gn_g=rwkv_gn_g, rwkv_gn_b=rwkv_gn_b)
    head = np.arange(BRANCH_W) // RWKV_HEAD
    ones = jnp.asarray(head[:, None] == head[None, :], BF16)
    cos_d, sin_d = _rope_tables(n_ctx, n_lat, DIFF_HALF, 2)
    cos_g, sin_g = _rope_tables(n_ctx, n_lat, GQA_HEAD, 1)

    cvec = jnp.concatenate([c, c_ctx[None, :], jnp.zeros((MOD_ROWS - B - 1, D_MODEL), c.dtype)], axis=0)
    mod = _modulation(cvec, w_mod, b_mod)
    xa = jnp.concatenate([ctx, x], axis=1)
    tm = 2176 if (B * N) % 2176 == 0 else TOK

    h = _prenorm(xa, mod[0], norm_pre_g[0])
    for li in range(depth):
        h = h.reshape(B * N, D_MODEL)
        pa = _project(h, w_in_b, li, O_A, O_LORA - O_A, 1024, BF16, tm, (B, N, O_LORA - O_A))
        pb = _project(h, w_in_b, li, O_B, O_C - O_B, 768, BF16, tm, (B, N, O_C - O_B))
        pc = _project(h, w_in_b, li, O_C, D_IN - O_C, 512, BF16, tm, (B, N, D_IN - O_C))
        lora = _project(h, w_in_b, li, O_LORA, LORA_PAD, LORA_PAD, F32, tm, (B, N, LORA_PAD))
        lam_init = 0.8 - 0.6 * math.exp(-0.3 * li)
        br_conv = _conv_branch(pa, conv_w[li], conv_b[li], conv_ln_g[li], conv_ln_b[li])
        br_rwkv = _rwkv_branch(pa, pb, lora, li, params, ones, n_ctx)
        br_diff = _diff_branch(pb, diff_lam[li], diff_subln_g[li], cos_d, sin_d, n_ctx, lam_init)
        br_gqa = _gqa_branch(pb, gqa_qk_norm_g[li], cos_g, sin_g, n_ctx)
        nxt = min(li + 1, depth - 1)
        xa, h = _merge((br_conv, br_rwkv, br_diff, br_gqa), pc, w_branch_b, b_gate, w_out_b, li, xa,
                       norm_post_g[li], mod[li], mod[nxt], norm_pre_g[nxt])
    return xa[:, n_ctx:]
```
